```python
import math
import jax, jax.numpy as jnp
from jax import lax
import numpy as np

D_MODEL = 1024
BATCH = 16
SEQ = 4096
DEPTH = 2

EPS = 1e-6
SSD_HEADS = 16
SSD_HEAD_DIM = 64
SSD_WIDTH = SSD_HEADS * SSD_HEAD_DIM
SSD_GROUPS = 2
SSD_STATE = 128
SSD_CONV = 4
SSD_CHUNK = 128
SSD_CONV_CH = SSD_WIDTH + 2 * SSD_GROUPS * SSD_STATE
DT_MIN = 1e-3
DT_MAX = 1e-1
POOL_GROUPS = 4
POOL_GROUP_DIM = 128
POOL_WIDTH = POOL_GROUPS * POOL_GROUP_DIM
POOL_WINDOWS = (2, 4, 8, 16)
MLA_HEADS = 8
MLA_Q_RANK = 384
MLA_KV_RANK = 256
MLA_NOPE = 64
MLA_ROPE = 32
MLA_V = 64
MLA_QK = MLA_NOPE + MLA_ROPE
MLA_WIDTH = MLA_HEADS * MLA_V
ROPE_THETA = 10000.0
Q_BLOCK = 128
MIX_WIDTH = SSD_WIDTH + POOL_WIDTH + MLA_WIDTH
IN_SIZES = (SSD_WIDTH, SSD_CONV_CH, SSD_HEADS, POOL_WIDTH, MLA_Q_RANK, MLA_KV_RANK, MLA_ROPE)
IN_COLS = SSD_WIDTH + SSD_CONV_CH + SSD_HEADS + POOL_WIDTH + MLA_Q_RANK + MLA_KV_RANK + MLA_ROPE
D_FF = 2816
FFN_CONV = 3

kernel_name = "hybrid_ssd_pool_mla_convffn"


def rmsnorm(x, w):
    xf = x.astype(jnp.float32)
    var = jnp.mean(xf * xf, axis=-1, keepdims=True)
    return (xf * lax.rsqrt(var + EPS)).astype(x.dtype) * w


def causal_dwconv(x, w, b):
    K = w.shape[0]
    S = x.shape[1]
    xp = jnp.pad(x, ((0, 0), (K - 1, 0), (0, 0)))
    acc = xp[:, 0:S] * w[0] + b
    for k in range(1, K):
        acc = acc + xp[:, k:k + S] * w[k]
    return acc


def rope(x, cos, sin):
    x1, x2 = jnp.split(x, 2, axis=-1)
    return jnp.concatenate([x1 * cos - x2 * sin, x1 * sin + x2 * cos], axis=-1)


def rope_tables(positions):
    inv_freq = ROPE_THETA ** (-jnp.arange(0, MLA_ROPE, 2, dtype=jnp.float32) / MLA_ROPE)
    ang = positions.astype(jnp.float32)[..., None] * inv_freq
    return jnp.cos(ang), jnp.sin(ang)


def ssd_mixer(z, xbc, dt_raw, conv_w, conv_b, dt_bias, a_log, d_skip, norm_w):
    Bsz, S, _ = xbc.shape
    G, E, P, N, L = SSD_GROUPS, SSD_HEADS // SSD_GROUPS, SSD_HEAD_DIM, SSD_STATE, SSD_CHUNK
    nc = S // L
    xbc = jax.nn.silu(causal_dwconv(xbc, conv_w, conv_b))
    xs, bs, cs = jnp.split(xbc, [SSD_WIDTH, SSD_WIDTH + G * N], axis=-1)
    xs = xs.reshape(Bsz, nc, L, G, E, P)
    bs = bs.reshape(Bsz, nc, L, G, N)
    cs = cs.reshape(Bsz, nc, L, G, N)
    dt = jax.nn.softplus((dt_raw + dt_bias).astype(jnp.float32)).reshape(Bsz, nc, L, G, E)
    a = -jnp.exp(a_log.astype(jnp.float32)).reshape(G, E)
    da = dt * a
    xdt = xs * dt[..., None]
    da_cum = jnp.cumsum(da, axis=2)
    causal = jnp.tril(jnp.ones((L, L), dtype=bool))
    seg = da_cum[:, :, :, None] - da_cum[:, :, None, :]
    decay = jnp.exp(jnp.where(causal[None, None, :, :, None, None], seg, -jnp.inf))
    cb = jnp.einsum("bclgn,bcsgn->bclsg", cs, bs)
    y_diag = jnp.einsum("bclsg,bclsge,bcsgep->bclgep", cb, decay, xdt)
    decay_to_end = jnp.exp(da_cum[:, :, -1:] - da_cum)
    chunk_states = jnp.einsum("bclgn,bclge,bclgep->bcgepn", bs, decay_to_end, xdt)
    chunk_decay = jnp.exp(da_cum[:, :, -1])

    def step(h, inp):
        dec, st = inp
        return h * dec[..., None, None] + st, h

    h0 = jnp.zeros((Bsz, G, E, P, N), dtype=chunk_states.dtype)
    _, h_in = lax.scan(step, h0, (jnp.moveaxis(chunk_decay, 1, 0), jnp.moveaxis(chunk_states, 1, 0)))
    h_in = jnp.moveaxis(h_in, 0, 1)
    y_off = jnp.einsum("bclgn,bcgepn,bclge->bclgep", cs, h_in, jnp.exp(da_cum))
    y = y_diag + y_off + xs * d_skip.reshape(G, E)[:, :, None]
    y = y.reshape(Bsz, S, SSD_WIDTH)
    return rmsnorm(y * jax.nn.silu(z), norm_w)


def pool_mixer(u, pool_w, pool_scale):
    Bsz, S, _ = u.shape
    uf = u.astype(jnp.float32)
    csum = jnp.pad(jnp.cumsum(uf, axis=1), ((0, 0), (1, 0), (0, 0)))
    count = jnp.arange(1, S + 1, dtype=jnp.float32)[:, None]
    means = []
    for gi, w in enumerate(POOL_WINDOWS):
        c = csum[:, :, gi * POOL_GROUP_DIM:(gi + 1) * POOL_GROUP_DIM]
        lag = jnp.pad(c, ((0, 0), (w - 1, 0), (0, 0)))[:, :S]
        means.append((c[:, 1:] - lag) / jnp.minimum(count, float(w)))
    pooled = (jnp.concatenate(means, axis=-1) - uf).astype(u.dtype)
    pooled = pooled.reshape(Bsz, S, POOL_GROUPS, POOL_GROUP_DIM)
    y = jnp.einsum("bsgc,gcd->bsgd", pooled, pool_w).reshape(Bsz, S, POOL_WIDTH)
    return y * pool_scale


def mla_mixer(c_q, c_kv, k_pe, cos, sin, q_norm, w_uq, kv_norm, w_ukv):
    Bsz, S, _ = c_q.shape
    H = MLA_HEADS
    q = (rmsnorm(c_q, q_norm) @ w_uq).reshape(Bsz, S, H, MLA_QK)
    kv = (rmsnorm(c_kv, kv_norm) @ w_ukv).reshape(Bsz, S, H, MLA_NOPE + MLA_V)
    q_nope, q_pe = jnp.split(q, [MLA_NOPE], axis=-1)
    k_nope, v = jnp.split(kv, [MLA_NOPE], axis=-1)
    q_pe = rope(q_pe, cos[:, :, None, :], sin[:, :, None, :])
    k_pe = rope(k_pe, cos, sin)
    q = jnp.concatenate([q_nope, q_pe], axis=-1)
    k = jnp.concatenate([k_nope, jnp.broadcast_to(k_pe[:, :, None, :], (Bsz, S, H, MLA_ROPE))], axis=-1)
    scale = 1.0 / math.sqrt(MLA_QK)
    nb = S // Q_BLOCK
    qb = jnp.moveaxis(q.reshape(Bsz, nb, Q_BLOCK, H, MLA_QK), 1, 0)
    key_pos = jnp.arange(S)

    def attend(args):
        q_blk, i = args
        s = jnp.einsum("bqhd,bkhd->bhqk", q_blk, k).astype(jnp.float32) * scale
        q_pos = i * Q_BLOCK + jnp.arange(Q_BLOCK)
        s = jnp.where(key_pos[None, :] <= q_pos[:, None], s, -jnp.inf)
        p = jax.nn.softmax(s, axis=-1).astype(v.dtype)
        return jnp.einsum("bhqk,bkhd->bqhd", p, v)

    o = lax.map(attend, (qb, jnp.arange(nb)))
    return jnp.moveaxis(o, 0, 1).reshape(Bsz, S, MLA_WIDTH)


def conv_ffn(h, w_up, conv_w, conv_b, w_down):
    up = causal_dwconv(h @ w_up, conv_w, conv_b)
    gate, val = jnp.split(up, 2, axis=-1)
    return (jax.nn.silu(gate) * val) @ w_down


def setup_inputs(seed: int = 0) -> dict:
    key = jax.random.key(seed)
    ks = jax.random.split(key, 24)
    f32 = jnp.float32

    def nrm(k, shape, scale):
        return jax.random.normal(k, shape, f32) * scale

    def gain(k, shape):
        return 1.0 + 0.02 * jax.random.normal(k, shape, f32)

    x = jax.random.normal(ks[0], (BATCH, SEQ, D_MODEL), f32)
    offsets = jax.random.randint(ks[1], (BATCH, 1), 0, 1024, dtype=jnp.int32)
    positions = (offsets + jnp.arange(SEQ, dtype=jnp.int32)[None, :]).astype(jnp.int32)
    u_dt = jax.random.uniform(ks[2], (DEPTH, SSD_HEADS), f32)
    dt0 = jnp.exp(u_dt * (math.log(DT_MAX) - math.log(DT_MIN)) + math.log(DT_MIN))
    dt_bias = dt0 + jnp.log(-jnp.expm1(-dt0))
    a_log = jnp.log(jax.random.uniform(ks[3], (DEPTH, SSD_HEADS), f32, 1.0, 16.0))
    return {
        "x": x,
        "positions": positions,
        "attn_norm": gain(ks[4], (DEPTH, D_MODEL)),
        "w_in": nrm(ks[5], (DEPTH, D_MODEL, IN_COLS), D_MODEL ** -0.5),
        "ssd_conv_w": nrm(ks[6], (DEPTH, SSD_CONV, SSD_CONV_CH), SSD_CONV ** -0.5),
        "ssd_conv_b": nrm(ks[7], (DEPTH, SSD_CONV_CH), 0.02),
        "ssd_dt_bias": dt_bias,
        "ssd_a_log": a_log,
        "ssd_d": 1.0 + 0.1 * jax.random.normal(ks[8], (DEPTH, SSD_HEADS), f32),
        "ssd_norm": gain(ks[9], (DEPTH, SSD_WIDTH)),
        "pool_w": nrm(ks[10], (DEPTH, POOL_GROUPS, POOL_GROUP_DIM, POOL_GROUP_DIM), POOL_GROUP_DIM ** -0.5),
        "pool_scale": gain(ks[11], (DEPTH, POOL_WIDTH)),
        "mla_q_norm": gain(ks[12], (DEPTH, MLA_Q_RANK)),
        "mla_w_uq": nrm(ks[13], (DEPTH, MLA_Q_RANK, MLA_HEADS * MLA_QK), MLA_Q_RANK ** -0.5),
        "mla_kv_norm": gain(ks[14], (DEPTH, MLA_KV_RANK)),
        "mla_w_ukv": nrm(ks[15], (DEPTH, MLA_KV_RANK, MLA_HEADS * (MLA_NOPE + MLA_V)), MLA_KV_RANK ** -0.5),
        "w_out": nrm(ks[16], (DEPTH, MIX_WIDTH, D_MODEL), MIX_WIDTH ** -0.5),
        "ffn_norm": gain(ks[17], (DEPTH, D_MODEL)),
        "ffn_w_up": nrm(ks[18], (DEPTH, D_MODEL, 2 * D_FF), D_MODEL ** -0.5),
        "ffn_conv_w": nrm(ks[19], (DEPTH, FFN_CONV, 2 * D_FF), FFN_CONV ** -0.5),
        "ffn_conv_b": nrm(ks[20], (DEPTH, 2 * D_FF), 0.02),
        "ffn_w_down": nrm(ks[21], (DEPTH, D_FF, D_MODEL), D_FF ** -0.5),
        "final_norm": gain(ks[22], (D_MODEL,)),
    }


def reference(x, positions, attn_norm, w_in, ssd_conv_w, ssd_conv_b, ssd_dt_bias, ssd_a_log,
              ssd_d, ssd_norm, pool_w, pool_scale, mla_q_norm, mla_w_uq, mla_kv_norm, mla_w_ukv,
              w_out, ffn_norm, ffn_w_up, ffn_conv_w, ffn_conv_b, ffn_w_down, final_norm):
    cos, sin = rope_tables(positions)
    splits = [int(s) for s in np.cumsum(IN_SIZES)[:-1]]
    for l in range(DEPTH):
        h = rmsnorm(x, attn_norm[l])
        proj = h @ w_in[l]
        z, xbc, dt_raw, u, c_q, c_kv, k_pe = jnp.split(proj, splits, axis=-1)
        y_ssd = ssd_mixer(z, xbc, dt_raw, ssd_conv_w[l], ssd_conv_b[l], ssd_dt_bias[l],
                          ssd_a_log[l], ssd_d[l], ssd_norm[l])
        y_pool = pool_mixer(u, pool_w[l], pool_scale[l])
        y_mla = mla_mixer(c_q, c_kv, k_pe, cos, sin, mla_q_norm[l], mla_w_uq[l],
                          mla_kv_norm[l], mla_w_ukv[l])
        x = x + jnp.concatenate([y_ssd, y_pool, y_mla], axis=-1) @ w_out[l]
        h = rmsnorm(x, ffn_norm[l])
        x = x + conv_ffn(h, ffn_w_up[l], ffn_conv_w[l], ffn_conv_b[l], ffn_w_down[l])
    return rmsnorm(x, final_norm)
```

```python
import functools
import math

import jax
import jax.numpy as jnp
import numpy as np
from jax import lax
from jax.experimental import pallas as pl
from jax.experimental.pallas import tpu as pltpu

F32 = jnp.float32
BF16 = jnp.bfloat16

EPS = 1e-6
SSD_HEADS = 16
SSD_HEAD_DIM = 64
SSD_WIDTH = SSD_HEADS * SSD_HEAD_DIM
SSD_GROUPS = 2
SSD_HEADS_PER_GROUP = SSD_HEADS // SSD_GROUPS
SSD_STATE = 128
SSD_CONV = 4
SSD_CHUNK = 128
SSD_CONV_CH = SSD_WIDTH + 2 * SSD_GROUPS * SSD_STATE
POOL_GROUPS = 4
POOL_GROUP_DIM = 128
POOL_WIDTH = POOL_GROUPS * POOL_GROUP_DIM
POOL_WINDOWS = (2, 4, 8, 16)
MLA_HEADS = 8
MLA_Q_RANK = 384
MLA_KV_RANK = 256
MLA_NOPE = 64
MLA_ROPE = 32
MLA_V = 64
MLA_QK = MLA_NOPE + MLA_ROPE
MLA_WIDTH = MLA_HEADS * MLA_V
ROPE_THETA = 10000.0
FFN_CONV = 3

LANES = 128
HALO = 16
MLA_HEAD_PAD = LANES
ROPE_LANE0 = MLA_NOPE
DT_LANE0 = 0
VMEM_LIMIT = 56 * 1024 * 1024

_C_Z = 0
_C_XBC = _C_Z + SSD_WIDTH
_C_U = _C_XBC + SSD_CONV_CH
_C_CQ = _C_U + POOL_WIDTH
_C_CKV = _C_CQ + MLA_Q_RANK
_C_SMALL = _C_CKV + MLA_KV_RANK
_C_END = _C_SMALL + LANES


def _sigmoid(x):
    return 1.0 / (1.0 + jnp.exp(-x))


def _silu(x):
    return x * _sigmoid(x)


def _rms_scale(x):
    return x * lax.rsqrt(jnp.mean(x * x, axis=-1, keepdims=True) + EPS)


def _row_tile(n, want):
    t = min(n, want)
    assert n % t == 0
    return t


def _params(*sem):
    return pltpu.CompilerParams(dimension_semantics=sem, vmem_limit_bytes=VMEM_LIMIT)


def _full(shape):
    zeros = (0,) * len(shape)
    return pl.BlockSpec(shape, lambda *_: zeros)


def _resident(shape):
    zeros = (0,) * len(shape)
    return pl.BlockSpec(shape, lambda *_: zeros, pipeline_mode=pl.Buffered(1))


def _rope_kernel(pos_ref, freq_ref, cos_ref, sin_ref):
    half = MLA_ROPE // 2
    pos = pos_ref[...].astype(F32)
    ang = freq_ref[...] * pos
    c = jnp.cos(ang)
    s = jnp.sin(ang)
    ts = pos.shape[-1]
    lo = jnp.zeros((ROPE_LANE0, ts), F32)
    hi = jnp.zeros((LANES - ROPE_LANE0 - 2 * half, ts), F32)
    cos_ref[...] = jnp.concatenate([lo, c, c, hi], axis=0).T
    sin_ref[...] = jnp.concatenate([lo, -s, s, hi], axis=0).T


def _rope_tables(positions):
    b, s = positions.shape
    ts = _row_tile(s, 512)
    half = MLA_ROPE // 2
    inv_freq = ROPE_THETA ** (-jnp.arange(0, MLA_ROPE, 2, dtype=F32) / MLA_ROPE)
    out = jax.ShapeDtypeStruct((b * s, LANES), F32)
    ns = s // ts
    return pl.pallas_call(
        _rope_kernel,
        grid=(b, ns),
        in_specs=[pl.BlockSpec((None, 1, ts), lambda i, j: (i, 0, j)), _full((half, 1))],
        out_specs=[pl.BlockSpec((ts, LANES), lambda i, j: (i * ns + j, 0))] * 2,
        out_shape=[out, out],
        compiler_params=_params("parallel", "parallel"),
        name="rope_tables",
    )(positions.reshape(b, 1, s), inv_freq.reshape(half, 1))


def _in_proj_kernel(x_ref, nw_ref, w_ref, z_ref, xbc_ref, u_ref, cq_ref, ckv_ref, small_ref):
    h = (_rms_scale(x_ref[...]) * nw_ref[...]).astype(BF16)

    def proj(a, b):
        return jnp.dot(h, w_ref[:, a:b], preferred_element_type=F32)

    z_ref[...] = proj(_C_Z, _C_XBC).astype(BF16)
    xbc_ref[...] = proj(_C_XBC, _C_U).astype(BF16)
    u_ref[...] = proj(_C_U, _C_CQ).astype(BF16)
    cq_ref[...] = proj(_C_CQ, _C_CKV).astype(BF16)
    ckv_ref[...] = proj(_C_CKV, _C_SMALL).astype(BF16)
    small_ref[...] = proj(_C_SMALL, _C_END)


def _pack_w_in(w_in):
    d = w_in.shape[0]
    o = 0
    parts = {}
    for name, n in (("z", SSD_WIDTH), ("xbc", SSD_CONV_CH), ("dt", SSD_HEADS), ("u", POOL_WIDTH),
                    ("cq", MLA_Q_RANK), ("ckv", MLA_KV_RANK), ("kpe", MLA_ROPE)):
        parts[name] = w_in[:, o:o + n]
        o += n
    small = jnp.concatenate([
        parts["dt"], jnp.zeros((d, ROPE_LANE0 - SSD_HEADS), F32),
        parts["kpe"], jnp.zeros((d, LANES - ROPE_LANE0 - MLA_ROPE), F32)], axis=1)
    return jnp.concatenate([parts["z"], parts["xbc"], parts["u"], parts["cq"], parts["ckv"], small],
                           axis=1).astype(BF16)


def _in_proj(x, norm_w, w_packed):
    t, d = x.shape
    tm = _row_tile(t, 512)
    widths = (SSD_WIDTH, SSD_CONV_CH, POOL_WIDTH, MLA_Q_RANK, MLA_KV_RANK)
    row = lambda n: pl.BlockSpec((tm, n), lambda i: (i, 0))
    return pl.pallas_call(
        _in_proj_kernel,
        grid=(t // tm,),
        in_specs=[row(d), _full((1, d)), _full((d, _C_END))],
        out_specs=[row(n) for n in widths] + [row(LANES)],
        out_shape=[jax.ShapeDtypeStruct((t, n), BF16) for n in widths]
        + [jax.ShapeDtypeStruct((t, LANES), F32)],
        compiler_params=_params("parallel"),
        name="in_proj",
    )(x, norm_w.reshape(1, d), w_packed)


def _expand_heads(m, lane):
    rows = m.shape[0]
    pieces = []
    for hp in range(SSD_HEADS // 2):
        lo = jnp.broadcast_to(m[:, 2 * hp:2 * hp + 1], (rows, LANES))
        hi = jnp.broadcast_to(m[:, 2 * hp + 1:2 * hp + 2], (rows, LANES))
        pieces.append(jnp.where(lane < SSD_HEAD_DIM, lo, hi))
    return jnp.concatenate(pieces, axis=1)


def _ssd_kernel(xbc_ref, z_ref, small_ref, cw_ref, cb_ref, dtb_ref, alog_ref, dskip_ref, nw_ref,
                y_ref, xe_scr, state_scr, yd_scr):
    L, N, P, E = SSD_CHUNK, SSD_STATE, SSD_HEAD_DIM, SSD_HEADS_PER_GROUP
    GW = E * P
    c = pl.program_id(1)

    @pl.when(c == 0)
    def _():
        xe_scr[0:8, :] = jnp.zeros((8, SSD_CONV_CH), F32)
        state_scr[...] = jnp.zeros_like(state_scr)

    xe_scr[8:8 + L, :] = xbc_ref[...].astype(F32)
    xe = xe_scr[...]
    cw = cw_ref[...]
    conv = cb_ref[...] + cw[SSD_CONV - 1:SSD_CONV] * xe[8:]
    for j in range(1, SSD_CONV):
        conv = conv + cw[SSD_CONV - 1 - j:SSD_CONV - j] * pltpu.roll(xe, j, axis=0)[8:]
    xe_scr[0:8, :] = xe[L:L + 8]
    act = _silu(conv)
    xs = act[:, :SSD_WIDTH]
    xs_b = xs.astype(BF16)

    lane = lax.broadcasted_iota(jnp.int32, (L, LANES), 1)
    row = lax.broadcasted_iota(jnp.int32, (L, LANES), 0)
    causal = lane <= row
    pre = small_ref[...] + dtb_ref[...]
    softplus = jnp.maximum(pre, 0.0) + jnp.log1p(jnp.exp(-jnp.abs(pre)))
    dt = jnp.where(lane < SSD_HEADS, softplus, 0.0)
    da = dt * (-jnp.exp(alog_ref[...]))
    tri = jnp.where(causal, 1.0, 0.0).astype(F32)
    cum = jnp.dot(tri, da, precision=lax.Precision.HIGHEST, preferred_element_type=F32)
    cum_t = cum.T
    dt_t = dt.T
    cum_last = cum[L - 1:L, :]
    ecum_x = _expand_heads(jnp.exp(cum), lane)
    xdt_end = (xs * _expand_heads(dt * jnp.exp(cum_last - cum), lane)).astype(BF16)

    y_off = []
    for g in range(SSD_GROUPS):
        b_g = act[:, SSD_WIDTH + g * N:SSD_WIDTH + (g + 1) * N]
        c_g = act[:, SSD_WIDTH + SSD_GROUPS * N + g * N:SSD_WIDTH + SSD_GROUPS * N + (g + 1) * N]
        c_gb = c_g.astype(BF16)
        cb = lax.dot_general(c_gb, b_g.astype(BF16), (((1,), (1,)), ((), ())),
                             preferred_element_type=F32)
        state = state_scr[g]
        y_off.append(jnp.dot(c_gb, state.astype(BF16), preferred_element_type=F32))
        for e in range(E):
            h = g * E + e
            seg = cum[:, h:h + 1] - cum_t[h:h + 1, :]
            decay = jnp.exp(jnp.where(causal, seg, -jnp.inf))
            m = (cb * decay * dt_t[h:h + 1, :]).astype(BF16)
            yd_scr[:, h * P:(h + 1) * P] = jnp.dot(m, xs_b[:, h * P:(h + 1) * P],
                                                   preferred_element_type=F32)
        new = jnp.dot(b_g.T.astype(BF16), xdt_end[:, g * GW:(g + 1) * GW],
                      preferred_element_type=F32)
        state_scr[g] = state * ecum_x[L - 1:L, g * GW:(g + 1) * GW] + new

    y = yd_scr[...] + jnp.concatenate(y_off, axis=1) * ecum_x + xs * dskip_ref[...]
    gated = y * _silu(z_ref[...].astype(F32))
    y_ref[...] = (_rms_scale(gated) * nw_ref[...]).astype(BF16)


def _ssd(xbc, z, small, conv_w, conv_b, dt_bias, a_log, d_skip, norm_w, batch, seq):
    L = SSD_CHUNK
    nc = seq // L
    pad = lambda v: jnp.pad(v, (0, LANES - SSD_HEADS)).reshape(1, LANES)
    row = lambda n: pl.BlockSpec((L, n), lambda b, c: (b * nc + c, 0))
    return pl.pallas_call(
        _ssd_kernel,
        grid=(batch, nc),
        in_specs=[row(SSD_CONV_CH), row(SSD_WIDTH), row(LANES),
                  _full((SSD_CONV, SSD_CONV_CH)), _full((1, SSD_CONV_CH)),
                  _full((1, LANES)), _full((1, LANES)), _full((1, SSD_WIDTH)), _full((1, SSD_WIDTH))],
        out_specs=row(SSD_WIDTH),
        out_shape=jax.ShapeDtypeStruct((batch * seq, SSD_WIDTH), BF16),
        scratch_shapes=[pltpu.VMEM((8 + L, SSD_CONV_CH), F32),
                        pltpu.VMEM((SSD_GROUPS, SSD_STATE, SSD_HEADS_PER_GROUP * SSD_HEAD_DIM), F32),
                        pltpu.VMEM((L, SSD_WIDTH), F32)],
        compiler_params=_params("parallel", "arbitrary"),
        name="ssd",
    )(xbc, z, small, conv_w, conv_b.reshape(1, -1), pad(dt_bias), pad(a_log),
      jnp.repeat(d_skip, SSD_HEAD_DIM).reshape(1, -1), norm_w.reshape(1, -1))


def _pool_kernel(u_ref, halo_ref, pw_ref, ps_ref, y_ref, *, tm, seq):
    start = (pl.program_id(0) * tm) % seq
    x = u_ref[...].astype(F32)
    halo = jnp.where(start == 0, 0.0, halo_ref[...].astype(F32))
    xe = jnp.concatenate([halo, x], axis=0)
    pos = start + lax.broadcasted_iota(jnp.int32, (tm, 1), 0)
    for gi, w in enumerate(POOL_WINDOWS):
        cols = slice(gi * POOL_GROUP_DIM, (gi + 1) * POOL_GROUP_DIM)
        acc = xe[:, cols]
        sh = 1
        while sh < w:
            acc = acc + pltpu.roll(acc, sh, axis=0)
            sh *= 2
        cnt = jnp.minimum(pos + 1, w).astype(F32)
        pooled = acc[HALO:] / cnt - x[:, cols]
        yg = jnp.dot(pooled.astype(BF16), pw_ref[gi], preferred_element_type=F32)
        y_ref[:, cols] = (yg * ps_ref[:, cols]).astype(BF16)


def _pool(u, pool_w, pool_scale, seq):
    t = u.shape[0]
    tm = _row_tile(seq, 512)
    assert max(POOL_WINDOWS) - 1 < HALO and tm % HALO == 0
    hb = tm // HALO
    return pl.pallas_call(
        functools.partial(_pool_kernel, tm=tm, seq=seq),
        grid=(t // tm,),
        in_specs=[pl.BlockSpec((tm, POOL_WIDTH), lambda i: (i, 0)),
                  pl.BlockSpec((HALO, POOL_WIDTH), lambda i: (jnp.maximum(i * hb - 1, 0), 0)),
                  _full((POOL_GROUPS, POOL_GROUP_DIM, POOL_GROUP_DIM)), _full((1, POOL_WIDTH))],
        out_specs=pl.BlockSpec((tm, POOL_WIDTH), lambda i: (i, 0)),
        out_shape=jax.ShapeDtypeStruct((t, POOL_WIDTH), BF16),
        compiler_params=_params("parallel"),
        name="pool",
    )(u, u, pool_w.astype(BF16), pool_scale.reshape(1, -1))


def _rotate_half(x, lane):
    half = MLA_ROPE // 2
    return jnp.where(lane < ROPE_LANE0 + half,
                     pltpu.roll(x, LANES - half, axis=1), pltpu.roll(x, half, axis=1))


def _mla_prep_kernel(cq_ref, ckv_ref, small_ref, cos_ref, sin_ref, qn_ref, kvn_ref,
                     wq_ref, wk_ref, wv_ref, q_ref, k_ref, v_ref, *, scale):
    tm = cq_ref.shape[0]
    lane = lax.broadcasted_iota(jnp.int32, (tm, LANES), 1)
    cos = cos_ref[...]
    sin = sin_ref[...]
    cos_q = jnp.where(lane < ROPE_LANE0, 1.0, cos)
    qn = (_rms_scale(cq_ref[...].astype(F32)) * qn_ref[...]).astype(BF16)
    kvn = (_rms_scale(ckv_ref[...].astype(F32)) * kvn_ref[...]).astype(BF16)
    q = jnp.dot(qn, wq_ref[...], preferred_element_type=F32)
    k = jnp.dot(kvn, wk_ref[...], preferred_element_type=F32)
    v_ref[...] = jnp.dot(kvn, wv_ref[...], preferred_element_type=F32).astype(BF16)
    kpe = small_ref[...]
    kpe = kpe * cos + _rotate_half(kpe, lane) * sin
    for h in range(MLA_HEADS):
        cols = slice(h * MLA_HEAD_PAD, (h + 1) * MLA_HEAD_PAD)
        qh = q[:, cols]
        qh = qh * cos_q + _rotate_half(qh, lane) * sin
        q_ref[:, cols] = (qh * scale).astype(BF16)
        k_ref[:, cols] = (k[:, cols] + kpe).astype(BF16)


def _pack_mla_weights(w_uq, w_ukv):
    rq = w_uq.shape[0]
    wq = w_uq.reshape(rq, MLA_HEADS, MLA_QK)
    wq = jnp.pad(wq, ((0, 0), (0, 0), (0, MLA_HEAD_PAD - MLA_QK))).reshape(rq, MLA_HEADS * MLA_HEAD_PAD)
    rk = w_ukv.shape[0]
    wkv = w_ukv.reshape(rk, MLA_HEADS, MLA_NOPE + MLA_V)
    wk = jnp.pad(wkv[:, :, :MLA_NOPE], ((0, 0), (0, 0), (0, MLA_HEAD_PAD - MLA_NOPE)))
    wk = wk.reshape(rk, MLA_HEADS * MLA_HEAD_PAD)
    wv = wkv[:, :, MLA_NOPE:].reshape(rk, MLA_WIDTH)
    return wq.astype(BF16), wk.astype(BF16), wv.astype(BF16)


def _mla_prep(cq, ckv, small, cos_t, sin_t, q_norm, kv_norm, wq, wk, wv):
    t = cq.shape[0]
    tm = _row_tile(t, 512)
    qk_w = MLA_HEADS * MLA_HEAD_PAD
    row = lambda n: pl.BlockSpec((tm, n), lambda i: (i, 0))
    return pl.pallas_call(
        functools.partial(_mla_prep_kernel, scale=1.0 / math.sqrt(MLA_QK)),
        grid=(t // tm,),
        in_specs=[row(MLA_Q_RANK), row(MLA_KV_RANK), row(LANES), row(LANES), row(LANES),
                  _full((1, MLA_Q_RANK)), _full((1, MLA_KV_RANK)),
                  _full((MLA_Q_RANK, qk_w)), _full((MLA_KV_RANK, qk_w)), _full((MLA_KV_RANK, MLA_WIDTH))],
        out_specs=[row(qk_w), row(qk_w), row(MLA_WIDTH)],
        out_shape=[jax.ShapeDtypeStruct((t, qk_w), BF16), jax.ShapeDtypeStruct((t, qk_w), BF16),
                   jax.ShapeDtypeStruct((t, MLA_WIDTH), BF16)],
        compiler_params=_params("parallel"),
        name="mla_prep",
    )(cq, ckv, small, cos_t, sin_t, q_norm.reshape(1, -1), kv_norm.reshape(1, -1), wq, wk, wv)


HEADS_PER_STEP = 2


def _flash_kernel(qi_ref, kj_ref, q_ref, k_ref, v_ref, o_ref, m_scr, l_scr, acc_scr):
    p = pl.program_id(2)
    i = qi_ref[p]
    j = kj_ref[p]
    tq, tk = q_ref.shape[0], k_ref.shape[0]

    @pl.when(j == 0)
    def _():
        m_scr[...] = jnp.full_like(m_scr, -1e30)
        l_scr[...] = jnp.zeros_like(l_scr)
        acc_scr[...] = jnp.zeros_like(acc_scr)

    def step(masked):
        for hh in range(HEADS_PER_STEP):
            q = q_ref[:, hh * MLA_HEAD_PAD:(hh + 1) * MLA_HEAD_PAD]
            k = k_ref[:, hh * MLA_HEAD_PAD:(hh + 1) * MLA_HEAD_PAD]
            v = v_ref[:, hh * MLA_V:(hh + 1) * MLA_V]
            s = lax.dot_general(q, k, (((1,), (1,)), ((), ())), preferred_element_type=F32)
            if masked:
                r = lax.broadcasted_iota(jnp.int32, (tq, tk), 0)
                c = lax.broadcasted_iota(jnp.int32, (tq, tk), 1)
                s = jnp.where(c <= r, s, -1e30)
            m_prev = m_scr[hh]
            m_new = jnp.maximum(m_prev, jnp.max(s, axis=-1, keepdims=True))
            alpha = jnp.exp(m_prev - m_new)
            pexp = jnp.exp(s - m_new)
            l_scr[hh] = alpha * l_scr[hh] + jnp.sum(pexp, axis=-1, keepdims=True)
            acc_scr[hh] = alpha * acc_scr[hh] + jnp.dot(pexp.astype(BF16), v, preferred_element_type=F32)
            m_scr[hh] = m_new

    @pl.when(j < i)
    def _():
        step(False)

    @pl.when(j == i)
    def _():
        step(True)
        for hh in range(HEADS_PER_STEP):
            o_ref[:, hh * MLA_V:(hh + 1) * MLA_V] = (acc_scr[hh] / l_scr[hh]).astype(BF16)


def _flash(q, k, v, batch, seq):
    tq = _row_tile(seq, 512)
    nq = seq // tq
    pairs = [(i, j) for i in range(nq) for j in range(i + 1)]
    qi = jnp.asarray(np.array([p[0] for p in pairs], np.int32))
    kj = jnp.asarray(np.array([p[1] for p in pairs], np.int32))
    qk_w = HEADS_PER_STEP * MLA_HEAD_PAD
    v_w = HEADS_PER_STEP * MLA_V
    grid_spec = pltpu.PrefetchScalarGridSpec(
        num_scalar_prefetch=2,
        grid=(batch, MLA_HEADS // HEADS_PER_STEP, len(pairs)),
        in_specs=[pl.BlockSpec((tq, qk_w), lambda b, h, p, qi, kj: (b * nq + qi[p], h)),
                  pl.BlockSpec((tq, qk_w), lambda b, h, p, qi, kj: (b * nq + kj[p], h)),
                  pl.BlockSpec((tq, v_w), lambda b, h, p, qi, kj: (b * nq + kj[p], h))],
        out_specs=pl.BlockSpec((tq, v_w), lambda b, h, p, qi, kj: (b * nq + qi[p], h)),
        scratch_shapes=[pltpu.VMEM((HEADS_PER_STEP, tq, 1), F32),
                        pltpu.VMEM((HEADS_PER_STEP, tq, 1), F32),
                        pltpu.VMEM((HEADS_PER_STEP, tq, MLA_V), F32)])
    return pl.pallas_call(
        _flash_kernel,
        grid_spec=grid_spec,
        out_shape=jax.ShapeDtypeStruct((batch * seq, MLA_WIDTH), BF16),
        compiler_params=_params("parallel", "parallel", "arbitrary"),
        name="flash_attn",
    )(qi, kj, q, k, v)


def _out_proj_kernel(ys_ref, yp_ref, ym_ref, x_ref, w_ref, nw_ref, xo_ref, h_ref):
    a, b = SSD_WIDTH, SSD_WIDTH + POOL_WIDTH
    acc = x_ref[...]
    acc = acc + jnp.dot(ys_ref[...], w_ref[0:a, :], preferred_element_type=F32)
    acc = acc + jnp.dot(yp_ref[...], w_ref[a:b, :], preferred_element_type=F32)
    acc = acc + jnp.dot(ym_ref[...], w_ref[b:, :], preferred_element_type=F32)
    xo_ref[...] = acc
    h_ref[...] = (_rms_scale(acc) * nw_ref[...]).astype(BF16)


def _out_proj(y_ssd, y_pool, y_mla, x, w_out, norm_w):
    t, d = x.shape
    tm = _row_tile(t, 512)
    row = lambda n: pl.BlockSpec((tm, n), lambda i: (i, 0))
    return pl.pallas_call(
        _out_proj_kernel,
        grid=(t // tm,),
        in_specs=[row(SSD_WIDTH), row(POOL_WIDTH), row(MLA_WIDTH), row(d),
                  _full(w_out.shape), _full((1, d))],
        out_specs=[row(d), row(d)],
        out_shape=[jax.ShapeDtypeStruct((t, d), F32), jax.ShapeDtypeStruct((t, d), BF16)],
        compiler_params=_params("parallel"),
        name="out_proj",
    )(y_ssd, y_pool, y_mla, x, w_out.astype(BF16), norm_w.reshape(1, d))


def _ffn_kernel(h_ref, halo_ref, x_ref, wup_ref, cw_ref, cb_ref, wdn_ref, fn_ref, o_ref,
                *, tm, seq, dff, chunk, final_norm):
    start = (pl.program_id(0) * tm) % seq
    halo = halo_ref[...]
    halo = jnp.where(start == 0, jnp.zeros_like(halo), halo)
    he = jnp.concatenate([halo, h_ref[...]], axis=0)
    acc = x_ref[...]

    def up_conv(c0, cn):
        u = jnp.dot(he, wup_ref[:, c0:c0 + cn], preferred_element_type=F32)
        w = cw_ref[:, c0:c0 + cn]
        y = cb_ref[:, c0:c0 + cn] + w[FFN_CONV - 1:FFN_CONV] * u
        for j in range(1, FFN_CONV):
            y = y + w[FFN_CONV - 1 - j:FFN_CONV - j] * pltpu.roll(u, j, axis=0)
        return y[HALO:]

    for c0 in range(0, dff, chunk):
        cn = min(chunk, dff - c0)
        gate = up_conv(c0, cn)
        val = up_conv(dff + c0, cn)
        a = (_silu(gate) * val).astype(BF16)
        acc = acc + jnp.dot(a, wdn_ref[c0:c0 + cn, :], preferred_element_type=F32)
    if final_norm:
        acc = _rms_scale(acc) * fn_ref[...]
    o_ref[...] = acc


def _ffn(h, x, w_up, conv_w, conv_b, w_down, final_w, seq, final_norm):
    t, d = x.shape
    dff = w_down.shape[0]
    tm = _row_tile(seq, 512)
    assert FFN_CONV - 1 < HALO and tm % HALO == 0
    hb = tm // HALO
    row = lambda n: pl.BlockSpec((tm, n), lambda i: (i, 0))
    return pl.pallas_call(
        functools.partial(_ffn_kernel, tm=tm, seq=seq, dff=dff, chunk=512, final_norm=final_norm),
        grid=(t // tm,),
        in_specs=[row(d), pl.BlockSpec((HALO, d), lambda i: (jnp.maximum(i * hb - 1, 0), 0)), row(d),
                  _resident((d, 2 * dff)), _full((FFN_CONV, 2 * dff)), _full((1, 2 * dff)),
                  _resident((dff, d)), _full((1, d))],
        out_specs=row(d),
        out_shape=jax.ShapeDtypeStruct((t, d), F32),
        compiler_params=_params("parallel"),
        name="ffn",
    )(h, h, x, w_up.astype(BF16), conv_w, conv_b.reshape(1, -1), w_down.astype(BF16), final_w.reshape(1, d))


def kernel(x, positions, attn_norm, w_in, ssd_conv_w, ssd_conv_b, ssd_dt_bias, ssd_a_log, ssd_d, ssd_norm,
           pool_w, pool_scale, mla_q_norm, mla_w_uq, mla_kv_norm, mla_w_ukv, w_out, ffn_norm, ffn_w_up,
           ffn_conv_w, ffn_conv_b, ffn_w_down, final_norm):
    batch, seq, d = x.shape
    depth = w_in.shape[0]
    assert seq % SSD_CHUNK == 0
    cos_t, sin_t = _rope_tables(positions)
    xf = x.reshape(batch * seq, d)
    for l in range(depth):
        z, xbc, u, cq, ckv, small = _in_proj(xf, attn_norm[l], _pack_w_in(w_in[l]))
        y_ssd = _ssd(xbc, z, small, ssd_conv_w[l], ssd_conv_b[l], ssd_dt_bias[l], ssd_a_log[l],
                     ssd_d[l], ssd_norm[l], batch, seq)
        y_pool = _pool(u, pool_w[l], pool_scale[l], seq)
        wq, wk, wv = _pack_mla_weights(mla_w_uq[l], mla_w_ukv[l])
        q, k, v = _mla_prep(cq, ckv, small, cos_t, sin_t, mla_q_norm[l], mla_kv_norm[l], wq, wk, wv)
        y_mla = _flash(q, k, v, batch, seq)
        xf, h = _out_proj(y_ssd, y_pool, y_mla, xf, w_out[l], ffn_norm[l])
        xf = _ffn(h, xf, ffn_w_up[l], ffn_conv_w[l], ffn_conv_b[l], ffn_w_down[l], final_norm, seq,
                  final_norm=(l == depth - 1))
    return xf.reshape(batch, seq, d)
```

```python
import functools
import math

import jax
import jax.numpy as jnp
import numpy as np
from jax import lax
from jax.experimental import pallas as pl
from jax.experimental.pallas import tpu as pltpu

F32 = jnp.float32
BF16 = jnp.bfloat16

EPS = 1e-6
SSD_HEADS = 16
SSD_HEAD_DIM = 64
SSD_WIDTH = SSD_HEADS * SSD_HEAD_DIM
SSD_GROUPS = 2
SSD_HEADS_PER_GROUP = SSD_HEADS // SSD_GROUPS
SSD_STATE = 128
SSD_CONV = 4
SSD_CHUNK = 128
SSD_CONV_CH = SSD_WIDTH + 2 * SSD_GROUPS * SSD_STATE
POOL_GROUPS = 4
POOL_GROUP_DIM = 128
POOL_WIDTH = POOL_GROUPS * POOL_GROUP_DIM
POOL_WINDOWS = (2, 4, 8, 16)
MLA_HEADS = 8
MLA_Q_RANK = 384
MLA_KV_RANK = 256
MLA_NOPE = 64
MLA_ROPE = 32
MLA_V = 64
MLA_QK = MLA_NOPE + MLA_ROPE
MLA_WIDTH = MLA_HEADS * MLA_V
ROPE_THETA = 10000.0
FFN_CONV = 3

LANES = 128
HALO = 16
MLA_HEAD_PAD = LANES
ROPE_LANE0 = MLA_NOPE
DT_LANE0 = 0
VMEM_LIMIT = 56 * 1024 * 1024

_C_Z = 0
_C_XBC = _C_Z + SSD_WIDTH
_C_U = _C_XBC + SSD_CONV_CH
_C_CQ = _C_U + POOL_WIDTH
_C_CKV = _C_CQ + MLA_Q_RANK
_C_SMALL = _C_CKV + MLA_KV_RANK
_C_END = _C_SMALL + LANES


def _sigmoid(x):
    return 1.0 / (1.0 + jnp.exp(-x))


def _silu(x):
    return x * _sigmoid(x)


def _rms_scale(x):
    return x * lax.rsqrt(jnp.mean(x * x, axis=-1, keepdims=True) + EPS)


def _row_tile(n, want):
    t = min(n, want)
    assert n % t == 0
    return t


def _params(*sem):
    return pltpu.CompilerParams(dimension_semantics=sem, vmem_limit_bytes=VMEM_LIMIT)


def _full(shape):
    zeros = (0,) * len(shape)
    return pl.BlockSpec(shape, lambda *_: zeros)


def _resident(shape):
    zeros = (0,) * len(shape)
    return pl.BlockSpec(shape, lambda *_: zeros, pipeline_mode=pl.Buffered(1))


def _rope_kernel(pos_ref, freq_ref, cos_ref, sin_ref):
    half = MLA_ROPE // 2
    pos = pos_ref[...].astype(F32)
    ang = freq_ref[...] * pos
    c = jnp.cos(ang)
    s = jnp.sin(ang)
    ts = pos.shape[-1]
    lo = jnp.zeros((ROPE_LANE0, ts), F32)
    hi = jnp.zeros((LANES - ROPE_LANE0 - 2 * half, ts), F32)
    cos_ref[...] = jnp.concatenate([lo, c, c, hi], axis=0).T
    sin_ref[...] = jnp.concatenate([lo, -s, s, hi], axis=0).T


def _rope_tables(positions):
    b, s = positions.shape
    ts = _row_tile(s, 512)
    half = MLA_ROPE // 2
    inv_freq = ROPE_THETA ** (-jnp.arange(0, MLA_ROPE, 2, dtype=F32) / MLA_ROPE)
    out = jax.ShapeDtypeStruct((b * s, LANES), F32)
    ns = s // ts
    return pl.pallas_call(
        _rope_kernel,
        grid=(b, ns),
        in_specs=[pl.BlockSpec((None, 1, ts), lambda i, j: (i, 0, j)), _full((half, 1))],
        out_specs=[pl.BlockSpec((ts, LANES), lambda i, j: (i * ns + j, 0))] * 2,
        out_shape=[out, out],
        compiler_params=_params("parallel", "parallel"),
        name="rope_tables",
    )(positions.reshape(b, 1, s), inv_freq.reshape(half, 1))


def _in_proj_kernel(x_ref, nw_ref, w_ref, z_ref, xbc_ref, u_ref, cq_ref, ckv_ref, small_ref):
    h = (_rms_scale(x_ref[...]) * nw_ref[...]).astype(BF16)

    def proj(a, b):
        return jnp.dot(h, w_ref[:, a:b], preferred_element_type=F32)

    z_ref[...] = proj(_C_Z, _C_XBC).astype(BF16)
    xbc_ref[...] = proj(_C_XBC, _C_U).astype(BF16)
    u_ref[...] = proj(_C_U, _C_CQ).astype(BF16)
    cq_ref[...] = proj(_C_CQ, _C_CKV).astype(BF16)
    ckv_ref[...] = proj(_C_CKV, _C_SMALL).astype(BF16)
    small_ref[...] = proj(_C_SMALL, _C_END)


def _pack_w_in(w_in):
    d = w_in.shape[0]
    o = 0
    parts = {}
    for name, n in (("z", SSD_WIDTH), ("xbc", SSD_CONV_CH), ("dt", SSD_HEADS), ("u", POOL_WIDTH),
                    ("cq", MLA_Q_RANK), ("ckv", MLA_KV_RANK), ("kpe", MLA_ROPE)):
        parts[name] = w_in[:, o:o + n]
        o += n
    small = jnp.concatenate([
        parts["dt"], jnp.zeros((d, ROPE_LANE0 - SSD_HEADS), F32),
        parts["kpe"], jnp.zeros((d, LANES - ROPE_LANE0 - MLA_ROPE), F32)], axis=1)
    return jnp.concatenate([parts["z"], parts["xbc"], parts["u"], parts["cq"], parts["ckv"], small],
                           axis=1).astype(BF16)


def _in_proj(x, norm_w, w_packed):
    t, d = x.shape
    tm = _row_tile(t, 512)
    widths = (SSD_WIDTH, SSD_CONV_CH, POOL_WIDTH, MLA_Q_RANK, MLA_KV_RANK)
    row = lambda n: pl.BlockSpec((tm, n), lambda i: (i, 0))
    return pl.pallas_call(
        _in_proj_kernel,
        grid=(t // tm,),
        in_specs=[row(d), _full((1, d)), _full((d, _C_END))],
        out_specs=[row(n) for n in widths] + [row(LANES)],
        out_shape=[jax.ShapeDtypeStruct((t, n), BF16) for n in widths]
        + [jax.ShapeDtypeStruct((t, LANES), F32)],
        compiler_params=_params("parallel"),
        name="in_proj",
    )(x, norm_w.reshape(1, d), w_packed)


def _expand_heads(m, lane):
    rows = m.shape[0]
    pieces = []
    for hp in range(SSD_HEADS // 2):
        lo = jnp.broadcast_to(m[:, 2 * hp:2 * hp + 1], (rows, LANES))
        hi = jnp.broadcast_to(m[:, 2 * hp + 1:2 * hp + 2], (rows, LANES))
        pieces.append(jnp.where(lane < SSD_HEAD_DIM, lo, hi))
    return jnp.concatenate(pieces, axis=1)


def _ssd_kernel(xbc_ref, z_ref, small_ref, cw_ref, cb_ref, dtb_ref, alog_ref, dskip_ref, nw_ref,
                y_ref, xe_scr, state_scr, yd_scr):
    L, N, P, E = SSD_CHUNK, SSD_STATE, SSD_HEAD_DIM, SSD_HEADS_PER_GROUP
    GW = E * P
    c = pl.program_id(1)

    @pl.when(c == 0)
    def _():
        xe_scr[0:8, :] = jnp.zeros((8, SSD_CONV_CH), F32)
        state_scr[...] = jnp.zeros_like(state_scr)

    xe_scr[8:8 + L, :] = xbc_ref[...].astype(F32)
    xe = xe_scr[...]
    cw = cw_ref[...]
    conv = cb_ref[...] + cw[SSD_CONV - 1:SSD_CONV] * xe[8:]
    for j in range(1, SSD_CONV):
        conv = conv + cw[SSD_CONV - 1 - j:SSD_CONV - j] * pltpu.roll(xe, j, axis=0)[8:]
    xe_scr[0:8, :] = xe[L:L + 8]
    act = _silu(conv)
    xs = act[:, :SSD_WIDTH]
    xs_b = xs.astype(BF16)

    lane = lax.broadcasted_iota(jnp.int32, (L, LANES), 1)
    row = lax.broadcasted_iota(jnp.int32, (L, LANES), 0)
    causal = lane <= row
    pre = small_ref[...] + dtb_ref[...]
    softplus = jnp.maximum(pre, 0.0) + jnp.log1p(jnp.exp(-jnp.abs(pre)))
    dt = jnp.where(lane < SSD_HEADS, softplus, 0.0)
    da = dt * (-jnp.exp(alog_ref[...]))
    tri = jnp.where(causal, 1.0, 0.0).astype(F32)
    cum = jnp.dot(tri, da, precision=lax.Precision.HIGHEST, preferred_element_type=F32)
    cum_t = cum.T
    dt_t = dt.T
    cum_last = cum[L - 1:L, :]
    ecum_x = _expand_heads(jnp.exp(cum), lane)
    xdt_end = (xs * _expand_heads(dt * jnp.exp(cum_last - cum), lane)).astype(BF16)

    y_off = []
    for g in range(SSD_GROUPS):
        b_g = act[:, SSD_WIDTH + g * N:SSD_WIDTH + (g + 1) * N]
        c_g = act[:, SSD_WIDTH + SSD_GROUPS * N + g * N:SSD_WIDTH + SSD_GROUPS * N + (g + 1) * N]
        c_gb = c_g.astype(BF16)
        cb = lax.dot_general(c_gb, b_g.astype(BF16), (((1,), (1,)), ((), ())),
                             preferred_element_type=F32)
        state = state_scr[g]
        y_off.append(jnp.dot(c_gb, state.astype(BF16), preferred_element_type=F32))
        for e in range(E):
            h = g * E + e
            seg = cum[:, h:h + 1] - cum_t[h:h + 1, :]
            decay = jnp.exp(jnp.where(causal, seg, -jnp.inf))
            m = (cb * decay * dt_t[h:h + 1, :]).astype(BF16)
            yd_scr[:, h * P:(h + 1) * P] = jnp.dot(m, xs_b[:, h * P:(h + 1) * P],
                                                   preferred_element_type=F32)
        new = jnp.dot(b_g.T.astype(BF16), xdt_end[:, g * GW:(g + 1) * GW],
                      preferred_element_type=F32)
        state_scr[g] = state * ecum_x[L - 1:L, g * GW:(g + 1) * GW] + new

    y = yd_scr[...] + jnp.concatenate(y_off, axis=1) * ecum_x + xs * dskip_ref[...]
    gated = y * _silu(z_ref[...].astype(F32))
    y_ref[...] = (_rms_scale(gated) * nw_ref[...]).astype(BF16)


def _ssd(xbc, z, small, conv_w, conv_b, dt_bias, a_log, d_skip, norm_w, batch, seq):
    L = SSD_CHUNK
    nc = seq // L
    pad = lambda v: jnp.pad(v, (0, LANES - SSD_HEADS)).reshape(1, LANES)
    row = lambda n: pl.BlockSpec((L, n), lambda b, c: (b * nc + c, 0))
    return pl.pallas_call(
        _ssd_kernel,
        grid=(batch, nc),
        in_specs=[row(SSD_CONV_CH), row(SSD_WIDTH), row(LANES),
                  _full((SSD_CONV, SSD_CONV_CH)), _full((1, SSD_CONV_CH)),
                  _full((1, LANES)), _full((1, LANES)), _full((1, SSD_WIDTH)), _full((1, SSD_WIDTH))],
        out_specs=row(SSD_WIDTH),
        out_shape=jax.ShapeDtypeStruct((batch * seq, SSD_WIDTH), BF16),
        scratch_shapes=[pltpu.VMEM((8 + L, SSD_CONV_CH), F32),
                        pltpu.VMEM((SSD_GROUPS, SSD_STATE, SSD_HEADS_PER_GROUP * SSD_HEAD_DIM), F32),
                        pltpu.VMEM((L, SSD_WIDTH), F32)],
        compiler_params=_params("parallel", "arbitrary"),
        name="ssd",
    )(xbc, z, small, conv_w, conv_b.reshape(1, -1), pad(dt_bias), pad(a_log),
      jnp.repeat(d_skip, SSD_HEAD_DIM).reshape(1, -1), norm_w.reshape(1, -1))


def _pool_kernel(u_ref, halo_ref, pw_ref, ps_ref, y_ref, *, tm, seq):
    start = (pl.program_id(0) * tm) % seq
    x = u_ref[...].astype(F32)
    halo = jnp.where(start == 0, 0.0, halo_ref[...].astype(F32))
    xe = jnp.concatenate([halo, x], axis=0)
    pos = start + lax.broadcasted_iota(jnp.int32, (tm, 1), 0)
    for gi, w in enumerate(POOL_WINDOWS):
        cols = slice(gi * POOL_GROUP_DIM, (gi + 1) * POOL_GROUP_DIM)
        acc = xe[:, cols]
        sh = 1
        while sh < w:
            acc = acc + pltpu.roll(acc, sh, axis=0)
            sh *= 2
        cnt = jnp.minimum(pos + 1, w).astype(F32)
        pooled = acc[HALO:] / cnt - x[:, cols]
        yg = jnp.dot(pooled.astype(BF16), pw_ref[gi], preferred_element_type=F32)
        y_ref[:, cols] = (yg * ps_ref[:, cols]).astype(BF16)


def _pool(u, pool_w, pool_scale, seq):
    t = u.shape[0]
    tm = _row_tile(seq, 512)
    assert max(POOL_WINDOWS) - 1 < HALO and tm % HALO == 0
    hb = tm // HALO
    return pl.pallas_call(
        functools.partial(_pool_kernel, tm=tm, seq=seq),
        grid=(t // tm,),
        in_specs=[pl.BlockSpec((tm, POOL_WIDTH), lambda i: (i, 0)),
                  pl.BlockSpec((HALO, POOL_WIDTH), lambda i: (jnp.maximum(i * hb - 1, 0), 0)),
                  _full((POOL_GROUPS, POOL_GROUP_DIM, POOL_GROUP_DIM)), _full((1, POOL_WIDTH))],
        out_specs=pl.BlockSpec((tm, POOL_WIDTH), lambda i: (i, 0)),
        out_shape=jax.ShapeDtypeStruct((t, POOL_WIDTH), BF16),
        compiler_params=_params("parallel"),
        name="pool",
    )(u, u, pool_w.astype(BF16), pool_scale.reshape(1, -1))


def _rotate_half(x, lane):
    half = MLA_ROPE // 2
    return jnp.where(lane < ROPE_LANE0 + half,
                     pltpu.roll(x, LANES - half, axis=1), pltpu.roll(x, half, axis=1))


def _mla_prep_kernel(cq_ref, ckv_ref, small_ref, cos_ref, sin_ref, qn_ref, kvn_ref,
                     wq_ref, wk_ref, wvt_ref, q_ref, k_ref, vt_ref, *, scale):
    tm = cq_ref.shape[0]
    lane = lax.broadcasted_iota(jnp.int32, (tm, LANES), 1)
    cos = cos_ref[...]
    sin = sin_ref[...]
    cos_q = jnp.where(lane < ROPE_LANE0, 1.0, cos)
    qn = (_rms_scale(cq_ref[...].astype(F32)) * qn_ref[...]).astype(BF16)
    kvn = (_rms_scale(ckv_ref[...].astype(F32)) * kvn_ref[...]).astype(BF16)
    q = jnp.dot(qn, wq_ref[...], preferred_element_type=F32)
    k = jnp.dot(kvn, wk_ref[...], preferred_element_type=F32)
    vt_ref[...] = lax.dot_general(wvt_ref[...], kvn, (((1,), (1,)), ((), ())),
                                  preferred_element_type=F32).astype(BF16)
    kpe = small_ref[...]
    kpe = kpe * cos + _rotate_half(kpe, lane) * sin
    for h in range(MLA_HEADS):
        cols = slice(h * MLA_HEAD_PAD, (h + 1) * MLA_HEAD_PAD)
        qh = q[:, cols]
        qh = qh * cos_q + _rotate_half(qh, lane) * sin
        q_ref[:, cols] = (qh * scale).astype(BF16)
        k_ref[:, cols] = (k[:, cols] + kpe).astype(BF16)


def _pack_mla_weights(w_uq, w_ukv):
    rq = w_uq.shape[0]
    wq = w_uq.reshape(rq, MLA_HEADS, MLA_QK)
    wq = jnp.pad(wq, ((0, 0), (0, 0), (0, MLA_HEAD_PAD - MLA_QK))).reshape(rq, MLA_HEADS * MLA_HEAD_PAD)
    rk = w_ukv.shape[0]
    wkv = w_ukv.reshape(rk, MLA_HEADS, MLA_NOPE + MLA_V)
    wk = jnp.pad(wkv[:, :, :MLA_NOPE], ((0, 0), (0, 0), (0, MLA_HEAD_PAD - MLA_NOPE)))
    wk = wk.reshape(rk, MLA_HEADS * MLA_HEAD_PAD)
    wvt = wkv[:, :, MLA_NOPE:].reshape(rk, MLA_WIDTH).T
    return wq.astype(BF16), wk.astype(BF16), wvt.astype(BF16)


def _mla_prep(cq, ckv, small, cos_t, sin_t, q_norm, kv_norm, wq, wk, wvt, batch, seq):
    t = cq.shape[0]
    tm = _row_tile(seq, 512)
    ns = seq // tm
    qk_w = MLA_HEADS * MLA_HEAD_PAD
    row = lambda n: pl.BlockSpec((tm, n), lambda i: (i, 0))
    return pl.pallas_call(
        functools.partial(_mla_prep_kernel, scale=math.log2(math.e) / math.sqrt(MLA_QK)),
        grid=(t // tm,),
        in_specs=[row(MLA_Q_RANK), row(MLA_KV_RANK), row(LANES), row(LANES), row(LANES),
                  _full((1, MLA_Q_RANK)), _full((1, MLA_KV_RANK)),
                  _full((MLA_Q_RANK, qk_w)), _full((MLA_KV_RANK, qk_w)), _full((MLA_WIDTH, MLA_KV_RANK))],
        out_specs=[row(qk_w), row(qk_w),
                   pl.BlockSpec((None, MLA_WIDTH, tm), lambda i: (i // ns, 0, i % ns))],
        out_shape=[jax.ShapeDtypeStruct((t, qk_w), BF16), jax.ShapeDtypeStruct((t, qk_w), BF16),
                   jax.ShapeDtypeStruct((batch, MLA_WIDTH, seq), BF16)],
        compiler_params=_params("parallel"),
        name="mla_prep",
    )(cq, ckv, small, cos_t, sin_t, q_norm.reshape(1, -1), kv_norm.reshape(1, -1), wq, wk, wvt)


HEADS_PER_STEP = 2


FLASH_ROWS = 32


def _flash_kernel(qi_ref, kj_ref, q_ref, k_ref, vt_ref, o_ref, m_scr, l_scr, acc_scr, s_scr, p_scr):
    p = pl.program_id(2)
    i = qi_ref[p]
    j = kj_ref[p]
    tq, tk = q_ref.shape[0], k_ref.shape[0]

    @pl.when(j == 0)
    def _():
        m_scr[...] = jnp.full_like(m_scr, -1e30)
        l_scr[...] = jnp.zeros_like(l_scr)
        acc_scr[...] = jnp.zeros_like(acc_scr)

    def step(masked):
        for hh in range(HEADS_PER_STEP):
            q = q_ref[:, hh * MLA_HEAD_PAD:(hh + 1) * MLA_HEAD_PAD]
            k = k_ref[:, hh * MLA_HEAD_PAD:(hh + 1) * MLA_HEAD_PAD]
            s = lax.dot_general(k, q, (((1,), (1,)), ((), ())), preferred_element_type=F32)
            if masked:
                key = lax.broadcasted_iota(jnp.int32, (tk, tq), 0)
                qry = lax.broadcasted_iota(jnp.int32, (tk, tq), 1)
                s = jnp.where(key <= qry, s, -1e30)
            s_scr[hh] = s
        for hh in range(HEADS_PER_STEP):
            vt = vt_ref[hh * MLA_V:(hh + 1) * MLA_V, :]
            m8 = None
            for r0 in range(0, tk, FLASH_ROWS):
                c8 = jnp.max(s_scr[hh, r0:r0 + FLASH_ROWS, :].reshape(FLASH_ROWS // 8, 8, tq), axis=0)
                m8 = c8 if m8 is None else jnp.maximum(m8, c8)
            m_prev = m_scr[hh]
            m_new = jnp.maximum(m_prev, jnp.max(m8, axis=0, keepdims=True))
            alpha = jnp.exp2(m_prev - m_new)
            l8 = jnp.zeros((8, tq), F32)
            for r0 in range(0, tk, FLASH_ROWS):
                pexp = jnp.exp2(s_scr[hh, r0:r0 + FLASH_ROWS, :] - m_new)
                l8 = l8 + jnp.sum(pexp.reshape(FLASH_ROWS // 8, 8, tq), axis=0)
                p_scr[hh, r0:r0 + FLASH_ROWS, :] = pexp.astype(BF16)
            l_scr[hh] = alpha * l_scr[hh] + jnp.sum(l8, axis=0, keepdims=True)
            acc_scr[hh] = alpha * acc_scr[hh] + jnp.dot(vt, p_scr[hh], preferred_element_type=F32)
            m_scr[hh] = m_new

    @pl.when(j < i)
    def _():
        step(False)

    @pl.when(j == i)
    def _():
        step(True)
        o_t = jnp.concatenate([acc_scr[hh] / l_scr[hh] for hh in range(HEADS_PER_STEP)], axis=0)
        o_ref[...] = o_t.T.astype(BF16)


def _flash(q, k, vt, batch, seq):
    tq = _row_tile(seq, 512)
    nq = seq // tq
    pairs = [(i, j) for i in range(nq) for j in range(i + 1)]
    qi = jnp.asarray(np.array([p[0] for p in pairs], np.int32))
    kj = jnp.asarray(np.array([p[1] for p in pairs], np.int32))
    qk_w = HEADS_PER_STEP * MLA_HEAD_PAD
    v_w = HEADS_PER_STEP * MLA_V
    grid_spec = pltpu.PrefetchScalarGridSpec(
        num_scalar_prefetch=2,
        grid=(batch, MLA_HEADS // HEADS_PER_STEP, len(pairs)),
        in_specs=[pl.BlockSpec((tq, qk_w), lambda b, h, p, qi, kj: (b * nq + qi[p], h)),
                  pl.BlockSpec((tq, qk_w), lambda b, h, p, qi, kj: (b * nq + kj[p], h)),
                  pl.BlockSpec((None, v_w, tq), lambda b, h, p, qi, kj: (b, h, kj[p]))],
        out_specs=pl.BlockSpec((tq, v_w), lambda b, h, p, qi, kj: (b * nq + qi[p], h)),
        scratch_shapes=[pltpu.VMEM((HEADS_PER_STEP, 1, tq), F32),
                        pltpu.VMEM((HEADS_PER_STEP, 1, tq), F32),
                        pltpu.VMEM((HEADS_PER_STEP, MLA_V, tq), F32),
                        pltpu.VMEM((HEADS_PER_STEP, tq, tq), F32),
                        pltpu.VMEM((HEADS_PER_STEP, tq, tq), BF16)])
    return pl.pallas_call(
        _flash_kernel,
        grid_spec=grid_spec,
        out_shape=jax.ShapeDtypeStruct((batch * seq, MLA_WIDTH), BF16),
        compiler_params=_params("parallel", "parallel", "arbitrary"),
        name="flash_attn",
    )(qi, kj, q, k, vt)


def _out_proj_kernel(ys_ref, yp_ref, ym_ref, x_ref, w_ref, nw_ref, xo_ref, h_ref):
    a, b = SSD_WIDTH, SSD_WIDTH + POOL_WIDTH
    acc = x_ref[...]
    acc = acc + jnp.dot(ys_ref[...], w_ref[0:a, :], preferred_element_type=F32)
    acc = acc + jnp.dot(yp_ref[...], w_ref[a:b, :], preferred_element_type=F32)
    acc = acc + jnp.dot(ym_ref[...], w_ref[b:, :], preferred_element_type=F32)
    xo_ref[...] = acc
    h_ref[...] = (_rms_scale(acc) * nw_ref[...]).astype(BF16)


def _out_proj(y_ssd, y_pool, y_mla, x, w_out, norm_w):
    t, d = x.shape
    tm = _row_tile(t, 512)
    row = lambda n: pl.BlockSpec((tm, n), lambda i: (i, 0))
    return pl.pallas_call(
        _out_proj_kernel,
        grid=(t // tm,),
        in_specs=[row(SSD_WIDTH), row(POOL_WIDTH), row(MLA_WIDTH), row(d),
                  _full(w_out.shape), _full((1, d))],
        out_specs=[row(d), row(d)],
        out_shape=[jax.ShapeDtypeStruct((t, d), F32), jax.ShapeDtypeStruct((t, d), BF16)],
        compiler_params=_params("parallel"),
        name="out_proj",
    )(y_ssd, y_pool, y_mla, x, w_out.astype(BF16), norm_w.reshape(1, d))


def _ffn_kernel(h_ref, halo_ref, x_ref, wup_ref, cw_ref, cb_ref, wdn_ref, fn_ref, o_ref,
                *, tm, seq, dff, chunk, final_norm):
    start = (pl.program_id(0) * tm) % seq
    halo = halo_ref[...]
    halo = jnp.where(start == 0, jnp.zeros_like(halo), halo)
    he = jnp.concatenate([halo, h_ref[...]], axis=0)
    acc = x_ref[...]

    def up_conv(c0, cn):
        u = jnp.dot(he, wup_ref[:, c0:c0 + cn], preferred_element_type=F32)
        w = cw_ref[:, c0:c0 + cn]
        y = cb_ref[:, c0:c0 + cn] + w[FFN_CONV - 1:FFN_CONV] * u
        for j in range(1, FFN_CONV):
            y = y + w[FFN_CONV - 1 - j:FFN_CONV - j] * pltpu.roll(u, j, axis=0)
        return y[HALO:]

    for c0 in range(0, dff, chunk):
        cn = min(chunk, dff - c0)
        gate = up_conv(c0, cn)
        val = up_conv(dff + c0, cn)
        a = (_silu(gate) * val).astype(BF16)
        acc = acc + jnp.dot(a, wdn_ref[c0:c0 + cn, :], preferred_element_type=F32)
    if final_norm:
        acc = _rms_scale(acc) * fn_ref[...]
    o_ref[...] = acc


def _ffn(h, x, w_up, conv_w, conv_b, w_down, final_w, seq, final_norm):
    t, d = x.shape
    dff = w_down.shape[0]
    tm = _row_tile(seq, 512)
    assert FFN_CONV - 1 < HALO and tm % HALO == 0
    hb = tm // HALO
    row = lambda n: pl.BlockSpec((tm, n), lambda i: (i, 0))
    return pl.pallas_call(
        functools.partial(_ffn_kernel, tm=tm, seq=seq, dff=dff, chunk=512, final_norm=final_norm),
        grid=(t // tm,),
        in_specs=[row(d), pl.BlockSpec((HALO, d), lambda i: (jnp.maximum(i * hb - 1, 0), 0)), row(d),
                  _resident((d, 2 * dff)), _full((FFN_CONV, 2 * dff)), _full((1, 2 * dff)),
                  _resident((dff, d)), _full((1, d))],
        out_specs=row(d),
        out_shape=jax.ShapeDtypeStruct((t, d), F32),
        compiler_params=_params("parallel"),
        name="ffn",
    )(h, h, x, w_up.astype(BF16), conv_w, conv_b.reshape(1, -1), w_down.astype(BF16), final_w.reshape(1, d))


def kernel(x, positions, attn_norm, w_in, ssd_conv_w, ssd_conv_b, ssd_dt_bias, ssd_a_log, ssd_d, ssd_norm,
           pool_w, pool_scale, mla_q_norm, mla_w_uq, mla_kv_norm, mla_w_ukv, w_out, ffn_norm, ffn_w_up,
           ffn_conv_w, ffn_conv_b, ffn_w_down, final_norm):
    batch, seq, d = x.shape
    depth = w_in.shape[0]
    assert seq % SSD_CHUNK == 0
    cos_t, sin_t = _rope_tables(positions)
    xf = x.reshape(batch * seq, d)
    for l in range(depth):
        z, xbc, u, cq, ckv, small = _in_proj(xf, attn_norm[l], _pack_w_in(w_in[l]))
        y_ssd = _ssd(xbc, z, small, ssd_conv_w[l], ssd_conv_b[l], ssd_dt_bias[l], ssd_a_log[l],
                     ssd_d[l], ssd_norm[l], batch, seq)
        y_pool = _pool(u, pool_w[l], pool_scale[l], seq)
        wq, wk, wvt = _pack_mla_weights(mla_w_uq[l], mla_w_ukv[l])
        q, k, vt = _mla_prep(cq, ckv, small, cos_t, sin_t, mla_q_norm[l], mla_kv_norm[l], wq, wk, wvt,
                             batch, seq)
        y_mla = _flash(q, k, vt, batch, seq)
        xf, h = _out_proj(y_ssd, y_pool, y_mla, xf, w_out[l], ffn_norm[l])
        xf = _ffn(h, xf, ffn_w_up[l], ffn_conv_w[l], ffn_conv_b[l], ffn_w_down[l], final_norm, seq,
                  final_norm=(l == depth - 1))
    return xf.reshape(batch, seq, d)
```

```python
import functools
import math

import jax
import jax.numpy as jnp
import numpy as np
from jax import lax
from jax.experimental import pallas as pl
from jax.experimental.pallas import tpu as pltpu

F32 = jnp.float32
BF16 = jnp.bfloat16

EPS = 1e-6
SSD_HEADS = 16
SSD_HEAD_DIM = 64
SSD_WIDTH = SSD_HEADS * SSD_HEAD_DIM
SSD_GROUPS = 2
SSD_HEADS_PER_GROUP = SSD_HEADS // SSD_GROUPS
SSD_STATE = 128
SSD_CONV = 4
SSD_CHUNK = 128
SSD_CONV_CH = SSD_WIDTH + 2 * SSD_GROUPS * SSD_STATE
POOL_GROUPS = 4
POOL_GROUP_DIM = 128
POOL_WIDTH = POOL_GROUPS * POOL_GROUP_DIM
POOL_WINDOWS = (2, 4, 8, 16)
MLA_HEADS = 8
MLA_Q_RANK = 384
MLA_KV_RANK = 256
MLA_NOPE = 64
MLA_ROPE = 32
MLA_V = 64
MLA_QK = MLA_NOPE + MLA_ROPE
MLA_WIDTH = MLA_HEADS * MLA_V
ROPE_THETA = 10000.0
FFN_CONV = 3

LANES = 128
HALO = 16
MLA_HEAD_PAD = LANES
ROPE_LANE0 = MLA_NOPE
DT_LANE0 = 0
VMEM_LIMIT = 56 * 1024 * 1024

_C_Z = 0
_C_XBC = _C_Z + SSD_WIDTH
_C_U = _C_XBC + SSD_CONV_CH
_C_CQ = _C_U + POOL_WIDTH
_C_CKV = _C_CQ + MLA_Q_RANK
_C_SMALL = _C_CKV + MLA_KV_RANK
_C_END = _C_SMALL + LANES


def _silu(x):
    h = 0.5 * x
    return h + h * jnp.tanh(h)


def _rms_scale(x):
    return x * lax.rsqrt(jnp.mean(x * x, axis=-1, keepdims=True) + EPS)


def _row_tile(n, want):
    t = min(n, want)
    assert n % t == 0
    return t


def _params(*sem, flags=None):
    return pltpu.CompilerParams(dimension_semantics=sem, vmem_limit_bytes=VMEM_LIMIT, flags=flags)


def _full(shape):
    zeros = (0,) * len(shape)
    return pl.BlockSpec(shape, lambda *_: zeros)


def _resident(shape):
    zeros = (0,) * len(shape)
    return pl.BlockSpec(shape, lambda *_: zeros, pipeline_mode=pl.Buffered(1))


def _rope_kernel(pos_ref, freq_ref, cos_ref, sin_ref):
    half = MLA_ROPE // 2
    pos = pos_ref[...].astype(F32)
    ang = freq_ref[...] * pos
    c = jnp.cos(ang)
    s = jnp.sin(ang)
    ts = pos.shape[-1]
    lo = jnp.zeros((ROPE_LANE0, ts), F32)
    hi = jnp.zeros((LANES - ROPE_LANE0 - 2 * half, ts), F32)
    cos_ref[...] = jnp.concatenate([lo, c, c, hi], axis=0).T
    sin_ref[...] = jnp.concatenate([lo, -s, s, hi], axis=0).T


def _rope_tables(positions):
    b, s = positions.shape
    ts = _row_tile(s, 512)
    half = MLA_ROPE // 2
    inv_freq = ROPE_THETA ** (-jnp.arange(0, MLA_ROPE, 2, dtype=F32) / MLA_ROPE)
    out = jax.ShapeDtypeStruct((b * s, LANES), F32)
    ns = s // ts
    return pl.pallas_call(
        _rope_kernel,
        grid=(b, ns),
        in_specs=[pl.BlockSpec((None, 1, ts), lambda i, j: (i, 0, j)), _full((half, 1))],
        out_specs=[pl.BlockSpec((ts, LANES), lambda i, j: (i * ns + j, 0))] * 2,
        out_shape=[out, out],
        compiler_params=_params("parallel", "parallel"),
        name="rope_tables",
    )(positions.reshape(b, 1, s), inv_freq.reshape(half, 1))


CONV_COLS = 512


def _in_proj_kernel(x_ref, nw_ref, w_ref, cw_ref, cb_ref, z_ref, xs_ref, bc_ref, u_ref, cq_ref, ckv_ref,
                    small_ref, tail_scr, *, tiles_per_seq):
    tm = x_ref.shape[0]

    @pl.when(pl.program_id(0) % tiles_per_seq == 0)
    def _():
        tail_scr[...] = jnp.zeros_like(tail_scr)

    h = (_rms_scale(x_ref[...]) * nw_ref[...]).astype(BF16)

    def proj(a, b):
        return jnp.dot(h, w_ref[:, a:b], preferred_element_type=F32)

    z_ref[...] = proj(_C_Z, _C_XBC).astype(BF16)
    for c0 in range(0, SSD_CONV_CH, CONV_COLS):
        cols = slice(c0, c0 + CONV_COLS)
        xbc = proj(_C_XBC + c0, _C_XBC + c0 + CONV_COLS)
        xe = jnp.concatenate([tail_scr[:, cols], xbc], axis=0)
        tail_scr[:, cols] = xbc[tm - 8:, :]
        cw = cw_ref[:, cols]
        conv = cb_ref[:, cols] + cw[SSD_CONV - 1:SSD_CONV] * xbc
        for j in range(1, SSD_CONV):
            conv = conv + cw[SSD_CONV - 1 - j:SSD_CONV - j] * pltpu.roll(xe, j, axis=0)[8:]
        act = _silu(conv).astype(BF16)
        if c0 + CONV_COLS <= SSD_WIDTH:
            xs_ref[:, cols] = act
        else:
            bc_ref[:, c0 - SSD_WIDTH:c0 - SSD_WIDTH + CONV_COLS] = act
    u_ref[...] = proj(_C_U, _C_CQ).astype(BF16)
    cq_ref[...] = proj(_C_CQ, _C_CKV).astype(BF16)
    ckv_ref[...] = proj(_C_CKV, _C_SMALL).astype(BF16)
    small_ref[...] = proj(_C_SMALL, _C_END)


def _pack_w_in(w_in):
    d = w_in.shape[0]
    o = 0
    parts = {}
    for name, n in (("z", SSD_WIDTH), ("xbc", SSD_CONV_CH), ("dt", SSD_HEADS), ("u", POOL_WIDTH),
                    ("cq", MLA_Q_RANK), ("ckv", MLA_KV_RANK), ("kpe", MLA_ROPE)):
        parts[name] = w_in[:, o:o + n]
        o += n
    small = jnp.concatenate([
        parts["dt"], jnp.zeros((d, ROPE_LANE0 - SSD_HEADS), F32),
        parts["kpe"], jnp.zeros((d, LANES - ROPE_LANE0 - MLA_ROPE), F32)], axis=1)
    return jnp.concatenate([parts["z"], parts["xbc"], parts["u"], parts["cq"], parts["ckv"], small],
                           axis=1).astype(BF16)


def _in_proj(x, norm_w, w_packed, conv_w, conv_b, seq):
    t, d = x.shape
    tm = _row_tile(seq, 512)
    assert SSD_WIDTH % CONV_COLS == 0 and SSD_CONV_CH % CONV_COLS == 0 and SSD_CONV - 1 < 8
    bc_w = SSD_CONV_CH - SSD_WIDTH
    widths = (SSD_WIDTH, SSD_WIDTH, bc_w, POOL_WIDTH, MLA_Q_RANK, MLA_KV_RANK)
    row = lambda n: pl.BlockSpec((tm, n), lambda i: (i, 0))
    return pl.pallas_call(
        functools.partial(_in_proj_kernel, tiles_per_seq=seq // tm),
        grid=(t // tm,),
        in_specs=[row(d), _full((1, d)), _full((d, _C_END)),
                  _full((SSD_CONV, SSD_CONV_CH)), _full((1, SSD_CONV_CH))],
        out_specs=[row(n) for n in widths] + [row(LANES)],
        out_shape=[jax.ShapeDtypeStruct((t, n), BF16) for n in widths]
        + [jax.ShapeDtypeStruct((t, LANES), F32)],
        scratch_shapes=[pltpu.VMEM((8, SSD_CONV_CH), F32)],
        compiler_params=_params("arbitrary"),
        name="in_proj",
    )(x, norm_w.reshape(1, d), w_packed, conv_w, conv_b.reshape(1, -1))


SSD_CHUNKS_PER_STEP = 2
SPLIT = 3


def _pack_split(v):
    out = None
    r = v
    for t in range(SPLIT):
        part = r.astype(BF16).astype(F32)
        r = r - part
        placed = part if t == 0 else pltpu.roll(part, t * SSD_HEADS, axis=1)
        out = placed if out is None else out + placed
    return out.astype(BF16)


def _unpack_split(r):
    out = r
    for t in range(1, SPLIT):
        out = out + pltpu.roll(r, LANES - t * SSD_HEADS, axis=1)
    return out


def _ssd_kernel(xs_ref, bc_ref, z_ref, small_ref, dtb_ref, alog_ref, dskip_ref, nw_ref, spread_ref, bcast_ref,
                y_ref, state_scr, yd_scr):
    L, N, P, E = SSD_CHUNK, SSD_STATE, SSD_HEAD_DIM, SSD_HEADS_PER_GROUP
    GW = E * P

    @pl.when(pl.program_id(1) == 0)
    def _():
        state_scr[...] = jnp.zeros_like(state_scr)

    lane = lax.broadcasted_iota(jnp.int32, (L, LANES), 1)
    row = lax.broadcasted_iota(jnp.int32, (L, LANES), 0)
    causal = lane <= row
    head_lane = lane < SSD_HEADS
    tri = jnp.where(causal, 1.0, 0.0).astype(BF16)
    a_neg = -jnp.exp(alog_ref[...])

    for sub in range(SSD_CHUNKS_PER_STEP):
        rows = slice(sub * L, (sub + 1) * L)
        xs_b = xs_ref[rows, :]
        xs = xs_b.astype(F32)
        pre = small_ref[rows, :] + dtb_ref[...]
        softplus = jnp.maximum(pre, 0.0) + jnp.log1p(jnp.exp(-jnp.abs(pre)))
        dt = jnp.where(head_lane, softplus, 0.0)
        da = dt * a_neg
        cum = jnp.where(head_lane, _unpack_split(
            jnp.dot(tri, _pack_split(da), preferred_element_type=F32)), 0.0)
        cum_t = cum.T
        cum_last = cum[L - 1:L, :]
        dte = jnp.where(head_lane, jnp.exp(cum_last - cum), 0.0)
        ecum = jnp.where(head_lane, jnp.exp(cum), 0.0)
        spread = jnp.dot(jnp.concatenate([_pack_split(dt), _pack_split(dte), _pack_split(ecum)], axis=0),
                         spread_ref[...], preferred_element_type=F32)
        dt_x, dte_x, ecum_x = spread[0:L], spread[L:2 * L], spread[2 * L:3 * L]
        cum_b = jnp.dot(_pack_split(cum), bcast_ref[...], preferred_element_type=F32)
        xdt = xs * dt_x
        xdt_b = xdt.astype(BF16)
        xdt_end = (xdt * dte_x).astype(BF16)

        y_off = []
        for g in range(SSD_GROUPS):
            b_g = bc_ref[rows, g * N:(g + 1) * N]
            c_g = bc_ref[rows, SSD_GROUPS * N + g * N:SSD_GROUPS * N + (g + 1) * N]
            cb = lax.dot_general(c_g, b_g, (((1,), (1,)), ((), ())), preferred_element_type=F32)
            state = state_scr[g]
            y_off.append(jnp.dot(c_g, state.astype(BF16), preferred_element_type=F32))
            for e in range(E):
                h = g * E + e
                seg = cum_b[:, h * L:(h + 1) * L] - cum_t[h:h + 1, :]
                decay = jnp.exp(jnp.where(causal, seg, -jnp.inf))
                yd_scr[:, h * P:(h + 1) * P] = jnp.dot((cb * decay).astype(BF16), xdt_b[:, h * P:(h + 1) * P],
                                                       preferred_element_type=F32)
            new = jnp.dot(b_g.astype(F32).T.astype(BF16), xdt_end[:, g * GW:(g + 1) * GW],
                          preferred_element_type=F32)
            state_scr[g] = state * ecum_x[L - 1:L, g * GW:(g + 1) * GW] + new

        y = yd_scr[...] + jnp.concatenate(y_off, axis=1) * ecum_x + xs * dskip_ref[...]
        gated = y * _silu(z_ref[rows, :].astype(F32))
        y_ref[rows, :] = (_rms_scale(gated) * nw_ref[...]).astype(BF16)


def _ssd(xs, bc, z, small, dt_bias, a_log, d_skip, norm_w, batch, seq):
    L = SSD_CHUNK
    rows = L * math.gcd(SSD_CHUNKS_PER_STEP, seq // L)
    assert rows == L * SSD_CHUNKS_PER_STEP and SPLIT * SSD_HEADS <= LANES
    ns = seq // rows
    pad = lambda v: jnp.pad(v, (0, LANES - SSD_HEADS)).reshape(1, LANES)
    k = np.arange(LANES)
    valid = (k < SPLIT * SSD_HEADS)[:, None]
    spread = (valid & ((k % SSD_HEADS)[:, None] == (np.arange(SSD_WIDTH) // SSD_HEAD_DIM)[None, :]))
    bcast = (valid & ((k % SSD_HEADS)[:, None] == (np.arange(SSD_HEADS * L) // L)[None, :]))
    row = lambda n: pl.BlockSpec((rows, n), lambda b, c: (b * ns + c, 0))
    return pl.pallas_call(
        _ssd_kernel,
        grid=(batch, ns),
        in_specs=[row(SSD_WIDTH), row(SSD_CONV_CH - SSD_WIDTH), row(SSD_WIDTH), row(LANES),
                  _full((1, LANES)), _full((1, LANES)), _full((1, SSD_WIDTH)), _full((1, SSD_WIDTH)),
                  _full((LANES, SSD_WIDTH)), _full((LANES, SSD_HEADS * L))],
        out_specs=row(SSD_WIDTH),
        out_shape=jax.ShapeDtypeStruct((batch * seq, SSD_WIDTH), BF16),
        scratch_shapes=[pltpu.VMEM((SSD_GROUPS, SSD_STATE, SSD_HEADS_PER_GROUP * SSD_HEAD_DIM), F32),
                        pltpu.VMEM((L, SSD_WIDTH), F32)],
        compiler_params=_params("parallel", "arbitrary"),
        name="ssd",
    )(xs, bc, z, small, pad(dt_bias), pad(a_log), jnp.repeat(d_skip, SSD_HEAD_DIM).reshape(1, -1),
      norm_w.reshape(1, -1), jnp.asarray(spread, BF16), jnp.asarray(bcast, BF16))


def _pool_kernel(u_ref, halo_ref, pw_ref, ps_ref, y_ref, *, tm, seq):
    start = (pl.program_id(0) * tm) % seq
    x = u_ref[...].astype(F32)
    halo = jnp.where(start == 0, 0.0, halo_ref[...].astype(F32))
    xe = jnp.concatenate([halo, x], axis=0)
    pos = start + lax.broadcasted_iota(jnp.int32, (tm, 1), 0)
    for gi, w in enumerate(POOL_WINDOWS):
        cols = slice(gi * POOL_GROUP_DIM, (gi + 1) * POOL_GROUP_DIM)
        acc = xe[:, cols]
        sh = 1
        while sh < w:
            acc = acc + pltpu.roll(acc, sh, axis=0)
            sh *= 2
        cnt = jnp.minimum(pos + 1, w).astype(F32)
        pooled = acc[HALO:] / cnt - x[:, cols]
        yg = jnp.dot(pooled.astype(BF16), pw_ref[gi], preferred_element_type=F32)
        y_ref[:, cols] = (yg * ps_ref[:, cols]).astype(BF16)


def _pool(u, pool_w, pool_scale, seq):
    t = u.shape[0]
    tm = _row_tile(seq, 512)
    assert max(POOL_WINDOWS) - 1 < HALO and tm % HALO == 0
    hb = tm // HALO
    return pl.pallas_call(
        functools.partial(_pool_kernel, tm=tm, seq=seq),
        grid=(t // tm,),
        in_specs=[pl.BlockSpec((tm, POOL_WIDTH), lambda i: (i, 0)),
                  pl.BlockSpec((HALO, POOL_WIDTH), lambda i: (jnp.maximum(i * hb - 1, 0), 0)),
                  _full((POOL_GROUPS, POOL_GROUP_DIM, POOL_GROUP_DIM)), _full((1, POOL_WIDTH))],
        out_specs=pl.BlockSpec((tm, POOL_WIDTH), lambda i: (i, 0)),
        out_shape=jax.ShapeDtypeStruct((t, POOL_WIDTH), BF16),
        compiler_params=_params("parallel"),
        name="pool",
    )(u, u, pool_w.astype(BF16), pool_scale.reshape(1, -1))


def _rotate_half(x, lane):
    half = MLA_ROPE // 2
    return jnp.where(lane < ROPE_LANE0 + half,
                     pltpu.roll(x, LANES - half, axis=1), pltpu.roll(x, half, axis=1))


def _mla_prep_kernel(cq_ref, ckv_ref, small_ref, cos_ref, sin_ref, qn_ref, kvn_ref,
                     wq_ref, wk_ref, wvt_ref, q_ref, k_ref, vt_ref, *, scale):
    tm = cq_ref.shape[0]
    lane = lax.broadcasted_iota(jnp.int32, (tm, LANES), 1)
    cos = cos_ref[...]
    sin = sin_ref[...]
    cos_q = jnp.where(lane < ROPE_LANE0, 1.0, cos)
    qn = (_rms_scale(cq_ref[...].astype(F32)) * qn_ref[...]).astype(BF16)
    kvn = (_rms_scale(ckv_ref[...].astype(F32)) * kvn_ref[...]).astype(BF16)
    q = jnp.dot(qn, wq_ref[...], preferred_element_type=F32)
    k = jnp.dot(kvn, wk_ref[...], preferred_element_type=F32)
    vt_ref[...] = lax.dot_general(wvt_ref[...], kvn, (((1,), (1,)), ((), ())),
                                  preferred_element_type=F32).astype(BF16)
    kpe = small_ref[...]
    kpe = kpe * cos + _rotate_half(kpe, lane) * sin
    for h in range(MLA_HEADS):
        cols = slice(h * MLA_HEAD_PAD, (h + 1) * MLA_HEAD_PAD)
        qh = q[:, cols]
        qh = qh * cos_q + _rotate_half(qh, lane) * sin
        q_ref[:, cols] = (qh * scale).astype(BF16)
        k_ref[:, cols] = (k[:, cols] + kpe).astype(BF16)


def _pack_mla_weights(w_uq, w_ukv):
    rq = w_uq.shape[0]
    wq = w_uq.reshape(rq, MLA_HEADS, MLA_QK)
    wq = jnp.pad(wq, ((0, 0), (0, 0), (0, MLA_HEAD_PAD - MLA_QK))).reshape(rq, MLA_HEADS * MLA_HEAD_PAD)
    rk = w_ukv.shape[0]
    wkv = w_ukv.reshape(rk, MLA_HEADS, MLA_NOPE + MLA_V)
    wk = jnp.pad(wkv[:, :, :MLA_NOPE], ((0, 0), (0, 0), (0, MLA_HEAD_PAD - MLA_NOPE)))
    wk = wk.reshape(rk, MLA_HEADS * MLA_HEAD_PAD)
    wvt = wkv[:, :, MLA_NOPE:].reshape(rk, MLA_WIDTH).T
    return wq.astype(BF16), wk.astype(BF16), wvt.astype(BF16)


def _mla_prep(cq, ckv, small, cos_t, sin_t, q_norm, kv_norm, wq, wk, wvt, batch, seq):
    t = cq.shape[0]
    tm = _row_tile(seq, 512)
    ns = seq // tm
    qk_w = MLA_HEADS * MLA_HEAD_PAD
    row = lambda n: pl.BlockSpec((tm, n), lambda i: (i, 0))
    return pl.pallas_call(
        functools.partial(_mla_prep_kernel, scale=math.log2(math.e) / math.sqrt(MLA_QK)),
        grid=(t // tm,),
        in_specs=[row(MLA_Q_RANK), row(MLA_KV_RANK), row(LANES), row(LANES), row(LANES),
                  _full((1, MLA_Q_RANK)), _full((1, MLA_KV_RANK)),
                  _full((MLA_Q_RANK, qk_w)), _full((MLA_KV_RANK, qk_w)), _full((MLA_WIDTH, MLA_KV_RANK))],
        out_specs=[row(qk_w), row(qk_w),
                   pl.BlockSpec((None, MLA_WIDTH, tm), lambda i: (i // ns, 0, i % ns))],
        out_shape=[jax.ShapeDtypeStruct((t, qk_w), BF16), jax.ShapeDtypeStruct((t, qk_w), BF16),
                   jax.ShapeDtypeStruct((batch, MLA_WIDTH, seq), BF16)],
        compiler_params=_params("parallel"),
        name="mla_prep",
    )(cq, ckv, small, cos_t, sin_t, q_norm.reshape(1, -1), kv_norm.reshape(1, -1), wq, wk, wvt)


HEADS_PER_STEP = 4


FLASH_ROWS = 32


def _flash_kernel(qi_ref, kj_ref, q_ref, k_ref, vt_ref, o_ref, m_scr, l_scr, acc_scr, s_scr, p_scr):
    p = pl.program_id(2)
    i = qi_ref[p]
    j = kj_ref[p]
    tq, tk = q_ref.shape[0], k_ref.shape[0]

    @pl.when(j == 0)
    def _():
        m_scr[...] = jnp.full_like(m_scr, -1e30)
        l_scr[...] = jnp.zeros_like(l_scr)
        acc_scr[...] = jnp.zeros_like(acc_scr)

    def step(masked):
        m_step = []
        for hh in range(HEADS_PER_STEP):
            q = q_ref[:, hh * MLA_HEAD_PAD:(hh + 1) * MLA_HEAD_PAD]
            k = k_ref[:, hh * MLA_HEAD_PAD:(hh + 1) * MLA_HEAD_PAD]
            s = lax.dot_general(k, q, (((1,), (1,)), ((), ())), preferred_element_type=F32)
            if masked:
                key = lax.broadcasted_iota(jnp.int32, (tk, tq), 0)
                qry = lax.broadcasted_iota(jnp.int32, (tk, tq), 1)
                s = jnp.where(key <= qry, s, -1e30)
            s_scr[hh] = s
            m_step.append(jnp.max(s, axis=0, keepdims=True))
        for hh in range(HEADS_PER_STEP):
            vt = vt_ref[hh * MLA_V:(hh + 1) * MLA_V, :]
            m_prev = m_scr[hh]
            m_new = jnp.maximum(m_prev, m_step[hh])
            alpha = jnp.exp2(m_prev - m_new)
            l8 = jnp.zeros((8, tq), F32)
            for r0 in range(0, tk, FLASH_ROWS):
                pexp = jnp.exp2(s_scr[hh, r0:r0 + FLASH_ROWS, :] - m_new)
                l8 = l8 + jnp.sum(pexp.reshape(FLASH_ROWS // 8, 8, tq), axis=0)
                p_scr[hh, r0:r0 + FLASH_ROWS, :] = pexp.astype(BF16)
            l_scr[hh] = alpha * l_scr[hh] + jnp.sum(l8, axis=0, keepdims=True)
            acc_scr[hh] = alpha * acc_scr[hh] + jnp.dot(vt, p_scr[hh], preferred_element_type=F32)
            m_scr[hh] = m_new

    @pl.when(j < i)
    def _():
        step(False)

    @pl.when(j == i)
    def _():
        step(True)
        o_t = jnp.concatenate([acc_scr[hh] / l_scr[hh] for hh in range(HEADS_PER_STEP)], axis=0)
        o_ref[...] = o_t.T.astype(BF16)


def _flash(q, k, vt, batch, seq):
    tq = _row_tile(seq, 512)
    nq = seq // tq
    pairs = [(i, j) for i in range(nq) for j in range(i + 1)]
    qi = jnp.asarray(np.array([p[0] for p in pairs], np.int32))
    kj = jnp.asarray(np.array([p[1] for p in pairs], np.int32))
    qk_w = HEADS_PER_STEP * MLA_HEAD_PAD
    v_w = HEADS_PER_STEP * MLA_V
    grid_spec = pltpu.PrefetchScalarGridSpec(
        num_scalar_prefetch=2,
        grid=(batch, MLA_HEADS // HEADS_PER_STEP, len(pairs)),
        in_specs=[pl.BlockSpec((tq, qk_w), lambda b, h, p, qi, kj: (b * nq + qi[p], h)),
                  pl.BlockSpec((tq, qk_w), lambda b, h, p, qi, kj: (b * nq + kj[p], h)),
                  pl.BlockSpec((None, v_w, tq), lambda b, h, p, qi, kj: (b, h, kj[p]))],
        out_specs=pl.BlockSpec((tq, v_w), lambda b, h, p, qi, kj: (b * nq + qi[p], h)),
        scratch_shapes=[pltpu.VMEM((HEADS_PER_STEP, 1, tq), F32),
                        pltpu.VMEM((HEADS_PER_STEP, 1, tq), F32),
                        pltpu.VMEM((HEADS_PER_STEP, MLA_V, tq), F32),
                        pltpu.VMEM((HEADS_PER_STEP, tq, tq), F32),
                        pltpu.VMEM((HEADS_PER_STEP, tq, tq), BF16)])
    return pl.pallas_call(
        _flash_kernel,
        grid_spec=grid_spec,
        out_shape=jax.ShapeDtypeStruct((batch * seq, MLA_WIDTH), BF16),
        compiler_params=_params("parallel", "parallel", "arbitrary"),
        name="flash_attn",
    )(qi, kj, q, k, vt)


def _out_proj_kernel(ys_ref, yp_ref, ym_ref, x_ref, w_ref, nw_ref, xo_ref, h_ref):
    a, b = SSD_WIDTH, SSD_WIDTH + POOL_WIDTH
    acc = x_ref[...]
    acc = acc + jnp.dot(ys_ref[...], w_ref[0:a, :], preferred_element_type=F32)
    acc = acc + jnp.dot(yp_ref[...], w_ref[a:b, :], preferred_element_type=F32)
    acc = acc + jnp.dot(ym_ref[...], w_ref[b:, :], preferred_element_type=F32)
    xo_ref[...] = acc
    h_ref[...] = (_rms_scale(acc) * nw_ref[...]).astype(BF16)


def _out_proj(y_ssd, y_pool, y_mla, x, w_out, norm_w):
    t, d = x.shape
    tm = _row_tile(t, 512)
    row = lambda n: pl.BlockSpec((tm, n), lambda i: (i, 0))
    return pl.pallas_call(
        _out_proj_kernel,
        grid=(t // tm,),
        in_specs=[row(SSD_WIDTH), row(POOL_WIDTH), row(MLA_WIDTH), row(d),
                  _full(w_out.shape), _full((1, d))],
        out_specs=[row(d), row(d)],
        out_shape=[jax.ShapeDtypeStruct((t, d), F32), jax.ShapeDtypeStruct((t, d), BF16)],
        compiler_params=_params("parallel"),
        name="out_proj",
    )(y_ssd, y_pool, y_mla, x, w_out.astype(BF16), norm_w.reshape(1, d))


def _ffn_kernel(h_ref, halo_ref, x_ref, wup_ref, cw_ref, cb_ref, wdn_ref, fn_ref, o_ref,
                *, tm, seq, dff, chunk, final_norm):
    start = (pl.program_id(0) * tm) % seq
    halo = halo_ref[...]
    halo = jnp.where(start == 0, jnp.zeros_like(halo), halo)
    he = jnp.concatenate([halo, h_ref[...]], axis=0)
    acc = x_ref[...]

    def up_conv(c0, cn):
        u = jnp.dot(he, wup_ref[:, c0:c0 + cn], preferred_element_type=F32)
        w = cw_ref[:, c0:c0 + cn]
        y = cb_ref[:, c0:c0 + cn] + w[FFN_CONV - 1:FFN_CONV] * u
        for j in range(1, FFN_CONV):
            y = y + w[FFN_CONV - 1 - j:FFN_CONV - j] * pltpu.roll(u, j, axis=0)
        return y[HALO:]

    for c0 in range(0, dff, chunk):
        cn = min(chunk, dff - c0)
        gate = up_conv(c0, cn)
        val = up_conv(dff + c0, cn)
        a = (_silu(gate) * val).astype(BF16)
        acc = acc + jnp.dot(a, wdn_ref[c0:c0 + cn, :], preferred_element_type=F32)
    if final_norm:
        acc = _rms_scale(acc) * fn_ref[...]
    o_ref[...] = acc


def _ffn(h, x, w_up, conv_w, conv_b, w_down, final_w, seq, final_norm):
    t, d = x.shape
    dff = w_down.shape[0]
    tm = _row_tile(seq, 512)
    assert FFN_CONV - 1 < HALO and tm % HALO == 0
    hb = tm // HALO
    row = lambda n: pl.BlockSpec((tm, n), lambda i: (i, 0))
    return pl.pallas_call(
        functools.partial(_ffn_kernel, tm=tm, seq=seq, dff=dff, chunk=512, final_norm=final_norm),
        grid=(t // tm,),
        in_specs=[row(d), pl.BlockSpec((HALO, d), lambda i: (jnp.maximum(i * hb - 1, 0), 0)), row(d),
                  _resident((d, 2 * dff)), _full((FFN_CONV, 2 * dff)), _full((1, 2 * dff)),
                  _resident((dff, d)), _full((1, d))],
        out_specs=row(d),
        out_shape=jax.ShapeDtypeStruct((t, d), F32),
        compiler_params=_params("parallel"),
        name="ffn",
    )(h, h, x, w_up.astype(BF16), conv_w, conv_b.reshape(1, -1), w_down.astype(BF16), final_w.reshape(1, d))


def kernel(x, positions, attn_norm, w_in, ssd_conv_w, ssd_conv_b, ssd_dt_bias, ssd_a_log, ssd_d, ssd_norm,
           pool_w, pool_scale, mla_q_norm, mla_w_uq, mla_kv_norm, mla_w_ukv, w_out, ffn_norm, ffn_w_up,
           ffn_conv_w, ffn_conv_b, ffn_w_down, final_norm):
    batch, seq, d = x.shape
    depth = w_in.shape[0]
    assert seq % SSD_CHUNK == 0
    cos_t, sin_t = _rope_tables(positions)
    xf = x.reshape(batch * seq, d)
    for l in range(depth):
        z, xs, bc, u, cq, ckv, small = _in_proj(xf, attn_norm[l], _pack_w_in(w_in[l]),
                                                ssd_conv_w[l], ssd_conv_b[l], seq)
        y_ssd = _ssd(xs, bc, z, small, ssd_dt_bias[l], ssd_a_log[l], ssd_d[l], ssd_norm[l], batch, seq)
        y_pool = _pool(u, pool_w[l], pool_scale[l], seq)
        wq, wk, wvt = _pack_mla_weights(mla_w_uq[l], mla_w_ukv[l])
        q, k, vt = _mla_prep(cq, ckv, small, cos_t, sin_t, mla_q_norm[l], mla_kv_norm[l], wq, wk, wvt,
                             batch, seq)
        y_mla = _flash(q, k, vt, batch, seq)
        xf, h = _out_proj(y_ssd, y_pool, y_mla, xf, w_out[l], ffn_norm[l])
        xf = _ffn(h, xf, ffn_w_up[l], ffn_conv_w[l], ffn_conv_b[l], ffn_w_down[l], final_norm, seq,
                  final_norm=(l == depth - 1))
    return xf.reshape(batch, seq, d)
```

```python
import functools
import math

import jax
import jax.numpy as jnp
import numpy as np
from jax import lax
from jax.experimental import pallas as pl
from jax.experimental.pallas import tpu as pltpu

F32 = jnp.float32
BF16 = jnp.bfloat16

EPS = 1e-6
SSD_HEADS = 16
SSD_HEAD_DIM = 64
SSD_WIDTH = SSD_HEADS * SSD_HEAD_DIM
SSD_GROUPS = 2
SSD_HEADS_PER_GROUP = SSD_HEADS // SSD_GROUPS
SSD_STATE = 128
SSD_CONV = 4
SSD_CHUNK = 128
SSD_CONV_CH = SSD_WIDTH + 2 * SSD_GROUPS * SSD_STATE
POOL_GROUPS = 4
POOL_GROUP_DIM = 128
POOL_WIDTH = POOL_GROUPS * POOL_GROUP_DIM
POOL_WINDOWS = (2, 4, 8, 16)
MLA_HEADS = 8
MLA_Q_RANK = 384
MLA_KV_RANK = 256
MLA_NOPE = 64
MLA_ROPE = 32
MLA_V = 64
MLA_QK = MLA_NOPE + MLA_ROPE
MLA_WIDTH = MLA_HEADS * MLA_V
ROPE_THETA = 10000.0
FFN_CONV = 3

LANES = 128
HALO = 16
MLA_HEAD_PAD = LANES
ROPE_LANE0 = MLA_NOPE
DT_LANE0 = 0
VMEM_LIMIT = 56 * 1024 * 1024

_C_Z = 0
_C_XBC = _C_Z + SSD_WIDTH
_C_U = _C_XBC + SSD_CONV_CH
_C_CQ = _C_U + POOL_WIDTH
_C_CKV = _C_CQ + MLA_Q_RANK
_C_SMALL = _C_CKV + MLA_KV_RANK
_C_END = _C_SMALL + LANES


def _silu(x):
    h = 0.5 * x
    return h + h * jnp.tanh(h)


def _rms_scale(x):
    return x * lax.rsqrt(jnp.mean(x * x, axis=-1, keepdims=True) + EPS)


def _row_tile(n, want):
    t = min(n, want)
    assert n % t == 0
    return t


def _params(*sem, flags=None):
    return pltpu.CompilerParams(dimension_semantics=sem, vmem_limit_bytes=VMEM_LIMIT, flags=flags)


def _full(shape):
    zeros = (0,) * len(shape)
    return pl.BlockSpec(shape, lambda *_: zeros)


def _resident(shape):
    zeros = (0,) * len(shape)
    return pl.BlockSpec(shape, lambda *_: zeros, pipeline_mode=pl.Buffered(1))


def _rope_kernel(pos_ref, freq_ref, cos_ref, sin_ref):
    half = MLA_ROPE // 2
    pos = pos_ref[...].astype(F32)
    ang = freq_ref[...] * pos
    c = jnp.cos(ang)
    s = jnp.sin(ang)
    ts = pos.shape[-1]
    lo = jnp.zeros((ROPE_LANE0, ts), F32)
    hi = jnp.zeros((LANES - ROPE_LANE0 - 2 * half, ts), F32)
    cos_ref[...] = jnp.concatenate([lo, c, c, hi], axis=0).T
    sin_ref[...] = jnp.concatenate([lo, -s, s, hi], axis=0).T


def _rope_tables(positions):
    b, s = positions.shape
    ts = _row_tile(s, 512)
    half = MLA_ROPE // 2
    inv_freq = ROPE_THETA ** (-jnp.arange(0, MLA_ROPE, 2, dtype=F32) / MLA_ROPE)
    out = jax.ShapeDtypeStruct((b * s, LANES), F32)
    ns = s // ts
    return pl.pallas_call(
        _rope_kernel,
        grid=(b, ns),
        in_specs=[pl.BlockSpec((None, 1, ts), lambda i, j: (i, 0, j)), _full((half, 1))],
        out_specs=[pl.BlockSpec((ts, LANES), lambda i, j: (i * ns + j, 0))] * 2,
        out_shape=[out, out],
        compiler_params=_params("parallel", "parallel"),
        name="rope_tables",
    )(positions.reshape(b, 1, s), inv_freq.reshape(half, 1))


def _in_proj_kernel(x_ref, nw_ref, w_ref, cw_ref, cb_ref, z_ref, xs_ref, bc_ref, u_ref, cq_ref, ckv_ref,
                    small_ref, tail_scr, *, tiles_per_seq):
    tm = x_ref.shape[0]

    @pl.when(pl.program_id(0) % tiles_per_seq == 0)
    def _():
        tail_scr[...] = jnp.zeros_like(tail_scr)

    h = (_rms_scale(x_ref[...]) * nw_ref[...]).astype(BF16)
    proj = jnp.dot(h, w_ref[...], preferred_element_type=F32)
    z_ref[...] = proj[:, _C_Z:_C_XBC].astype(BF16)
    xbc = proj[:, _C_XBC:_C_U]
    xe = jnp.concatenate([tail_scr[...], xbc], axis=0)
    tail_scr[...] = xbc[tm - 8:, :]
    cw = cw_ref[...]
    conv = cb_ref[...] + cw[SSD_CONV - 1:SSD_CONV] * xbc
    for j in range(1, SSD_CONV):
        conv = conv + cw[SSD_CONV - 1 - j:SSD_CONV - j] * pltpu.roll(xe, j, axis=0)[8:]
    act = _silu(conv).astype(BF16)
    xs_ref[...] = act[:, :SSD_WIDTH]
    bc_ref[...] = act[:, SSD_WIDTH:]
    u_ref[...] = proj[:, _C_U:_C_CQ].astype(BF16)
    cq_ref[...] = proj[:, _C_CQ:_C_CKV].astype(BF16)
    ckv_ref[...] = proj[:, _C_CKV:_C_SMALL].astype(BF16)
    small_ref[...] = proj[:, _C_SMALL:_C_END]


def _pack_w_in(w_in):
    d = w_in.shape[0]
    o = 0
    parts = {}
    for name, n in (("z", SSD_WIDTH), ("xbc", SSD_CONV_CH), ("dt", SSD_HEADS), ("u", POOL_WIDTH),
                    ("cq", MLA_Q_RANK), ("ckv", MLA_KV_RANK), ("kpe", MLA_ROPE)):
        parts[name] = w_in[:, o:o + n]
        o += n
    small = jnp.concatenate([
        parts["dt"], jnp.zeros((d, ROPE_LANE0 - SSD_HEADS), F32),
        parts["kpe"], jnp.zeros((d, LANES - ROPE_LANE0 - MLA_ROPE), F32)], axis=1)
    return jnp.concatenate([parts["z"], parts["xbc"], parts["u"], parts["cq"], parts["ckv"], small],
                           axis=1).astype(BF16)


def _in_proj(x, norm_w, w_packed, conv_w, conv_b, seq):
    t, d = x.shape
    tm = _row_tile(seq, 512)
    assert SSD_CONV - 1 < 8
    bc_w = SSD_CONV_CH - SSD_WIDTH
    widths = (SSD_WIDTH, SSD_WIDTH, bc_w, POOL_WIDTH, MLA_Q_RANK, MLA_KV_RANK)
    row = lambda n: pl.BlockSpec((tm, n), lambda i: (i, 0))
    return pl.pallas_call(
        functools.partial(_in_proj_kernel, tiles_per_seq=seq // tm),
        grid=(t // tm,),
        in_specs=[row(d), _full((1, d)), _full((d, _C_END)),
                  _full((SSD_CONV, SSD_CONV_CH)), _full((1, SSD_CONV_CH))],
        out_specs=[row(n) for n in widths] + [row(LANES)],
        out_shape=[jax.ShapeDtypeStruct((t, n), BF16) for n in widths]
        + [jax.ShapeDtypeStruct((t, LANES), F32)],
        scratch_shapes=[pltpu.VMEM((8, SSD_CONV_CH), F32)],
        compiler_params=_params("arbitrary"),
        name="in_proj",
    )(x, norm_w.reshape(1, d), w_packed, conv_w, conv_b.reshape(1, -1))


SSD_CHUNKS_PER_STEP = 2
SPLIT = 3


def _pack_split(v):
    out = None
    r = v
    for t in range(SPLIT):
        part = r.astype(BF16).astype(F32)
        r = r - part
        placed = part if t == 0 else pltpu.roll(part, t * SSD_HEADS, axis=1)
        out = placed if out is None else out + placed
    return out.astype(BF16)


def _unpack_split(r):
    out = r
    for t in range(1, SPLIT):
        out = out + pltpu.roll(r, LANES - t * SSD_HEADS, axis=1)
    return out


def _ssd_kernel(xs_ref, bc_ref, z_ref, small_ref, dtb_ref, alog_ref, dskip_ref, nw_ref, spread_ref, bcast_ref,
                y_ref, state_scr, yd_scr):
    L, N, P, E = SSD_CHUNK, SSD_STATE, SSD_HEAD_DIM, SSD_HEADS_PER_GROUP
    GW = E * P

    @pl.when(pl.program_id(1) == 0)
    def _():
        state_scr[...] = jnp.zeros_like(state_scr)

    lane = lax.broadcasted_iota(jnp.int32, (L, LANES), 1)
    row = lax.broadcasted_iota(jnp.int32, (L, LANES), 0)
    causal = lane <= row
    head_lane = lane < SSD_HEADS
    tri = jnp.where(causal, 1.0, 0.0).astype(BF16)
    a_neg = -jnp.exp(alog_ref[...])

    for sub in range(SSD_CHUNKS_PER_STEP):
        rows = slice(sub * L, (sub + 1) * L)
        xs_b = xs_ref[rows, :]
        xs = xs_b.astype(F32)
        pre = small_ref[rows, :] + dtb_ref[...]
        softplus = jnp.maximum(pre, 0.0) + jnp.log1p(jnp.exp(-jnp.abs(pre)))
        dt = jnp.where(head_lane, softplus, 0.0)
        da = dt * a_neg
        cum = jnp.where(head_lane, _unpack_split(
            jnp.dot(tri, _pack_split(da), preferred_element_type=F32)), 0.0)
        cum_t = cum.T
        cum_last = cum[L - 1:L, :]
        dte = jnp.where(head_lane, jnp.exp(cum_last - cum), 0.0)
        ecum = jnp.where(head_lane, jnp.exp(cum), 0.0)
        spread = jnp.dot(jnp.concatenate([_pack_split(dt), _pack_split(dte), _pack_split(ecum)], axis=0),
                         spread_ref[...], preferred_element_type=F32)
        dt_x, dte_x, ecum_x = spread[0:L], spread[L:2 * L], spread[2 * L:3 * L]
        cum_b = jnp.dot(_pack_split(cum), bcast_ref[...], preferred_element_type=F32)
        xdt = xs * dt_x
        xdt_b = xdt.astype(BF16)
        xdt_end = (xdt * dte_x).astype(BF16)

        y_off = []
        for g in range(SSD_GROUPS):
            b_g = bc_ref[rows, g * N:(g + 1) * N]
            c_g = bc_ref[rows, SSD_GROUPS * N + g * N:SSD_GROUPS * N + (g + 1) * N]
            cb = lax.dot_general(c_g, b_g, (((1,), (1,)), ((), ())), preferred_element_type=F32)
            state = state_scr[g]
            y_off.append(jnp.dot(c_g, state.astype(BF16), preferred_element_type=F32))
            for e in range(E):
                h = g * E + e
                seg = cum_b[:, h * L:(h + 1) * L] - cum_t[h:h + 1, :]
                decay = jnp.exp(jnp.where(causal, seg, -jnp.inf))
                yd_scr[:, h * P:(h + 1) * P] = jnp.dot((cb * decay).astype(BF16), xdt_b[:, h * P:(h + 1) * P],
                                                       preferred_element_type=F32)
            new = jnp.dot(b_g.astype(F32).T.astype(BF16), xdt_end[:, g * GW:(g + 1) * GW],
                          preferred_element_type=F32)
            state_scr[g] = state * ecum_x[L - 1:L, g * GW:(g + 1) * GW] + new

        y = yd_scr[...] + jnp.concatenate(y_off, axis=1) * ecum_x + xs * dskip_ref[...]
        gated = y * _silu(z_ref[rows, :].astype(F32))
        y_ref[rows, :] = (_rms_scale(gated) * nw_ref[...]).astype(BF16)


def _ssd(xs, bc, z, small, dt_bias, a_log, d_skip, norm_w, batch, seq):
    L = SSD_CHUNK
    rows = L * math.gcd(SSD_CHUNKS_PER_STEP, seq // L)
    assert rows == L * SSD_CHUNKS_PER_STEP and SPLIT * SSD_HEADS <= LANES
    ns = seq // rows
    pad = lambda v: jnp.pad(v, (0, LANES - SSD_HEADS)).reshape(1, LANES)
    k = np.arange(LANES)
    valid = (k < SPLIT * SSD_HEADS)[:, None]
    spread = (valid & ((k % SSD_HEADS)[:, None] == (np.arange(SSD_WIDTH) // SSD_HEAD_DIM)[None, :]))
    bcast = (valid & ((k % SSD_HEADS)[:, None] == (np.arange(SSD_HEADS * L) // L)[None, :]))
    row = lambda n: pl.BlockSpec((rows, n), lambda b, c: (b * ns + c, 0))
    return pl.pallas_call(
        _ssd_kernel,
        grid=(batch, ns),
        in_specs=[row(SSD_WIDTH), row(SSD_CONV_CH - SSD_WIDTH), row(SSD_WIDTH), row(LANES),
                  _full((1, LANES)), _full((1, LANES)), _full((1, SSD_WIDTH)), _full((1, SSD_WIDTH)),
                  _full((LANES, SSD_WIDTH)), _full((LANES, SSD_HEADS * L))],
        out_specs=row(SSD_WIDTH),
        out_shape=jax.ShapeDtypeStruct((batch * seq, SSD_WIDTH), BF16),
        scratch_shapes=[pltpu.VMEM((SSD_GROUPS, SSD_STATE, SSD_HEADS_PER_GROUP * SSD_HEAD_DIM), F32),
                        pltpu.VMEM((L, SSD_WIDTH), F32)],
        compiler_params=_params("parallel", "arbitrary"),
        name="ssd",
    )(xs, bc, z, small, pad(dt_bias), pad(a_log), jnp.repeat(d_skip, SSD_HEAD_DIM).reshape(1, -1),
      norm_w.reshape(1, -1), jnp.asarray(spread, BF16), jnp.asarray(bcast, BF16))


def _pool_kernel(u_ref, halo_ref, pw_ref, ps_ref, y_ref, *, tm, seq):
    start = (pl.program_id(0) * tm) % seq
    x = u_ref[...].astype(F32)
    halo = jnp.where(start == 0, 0.0, halo_ref[...].astype(F32))
    xe = jnp.concatenate([halo, x], axis=0)
    pos = start + lax.broadcasted_iota(jnp.int32, (tm, 1), 0)
    for gi, w in enumerate(POOL_WINDOWS):
        cols = slice(gi * POOL_GROUP_DIM, (gi + 1) * POOL_GROUP_DIM)
        acc = xe[:, cols]
        sh = 1
        while sh < w:
            acc = acc + pltpu.roll(acc, sh, axis=0)
            sh *= 2
        cnt = jnp.minimum(pos + 1, w).astype(F32)
        pooled = acc[HALO:] / cnt - x[:, cols]
        yg = jnp.dot(pooled.astype(BF16), pw_ref[gi], preferred_element_type=F32)
        y_ref[:, cols] = (yg * ps_ref[:, cols]).astype(BF16)


def _pool(u, pool_w, pool_scale, seq):
    t = u.shape[0]
    tm = _row_tile(seq, 512)
    assert max(POOL_WINDOWS) - 1 < HALO and tm % HALO == 0
    hb = tm // HALO
    return pl.pallas_call(
        functools.partial(_pool_kernel, tm=tm, seq=seq),
        grid=(t // tm,),
        in_specs=[pl.BlockSpec((tm, POOL_WIDTH), lambda i: (i, 0)),
                  pl.BlockSpec((HALO, POOL_WIDTH), lambda i: (jnp.maximum(i * hb - 1, 0), 0)),
                  _full((POOL_GROUPS, POOL_GROUP_DIM, POOL_GROUP_DIM)), _full((1, POOL_WIDTH))],
        out_specs=pl.BlockSpec((tm, POOL_WIDTH), lambda i: (i, 0)),
        out_shape=jax.ShapeDtypeStruct((t, POOL_WIDTH), BF16),
        compiler_params=_params("parallel"),
        name="pool",
    )(u, u, pool_w.astype(BF16), pool_scale.reshape(1, -1))


def _rotate_half(x, lane):
    half = MLA_ROPE // 2
    return jnp.where(lane < ROPE_LANE0 + half,
                     pltpu.roll(x, LANES - half, axis=1), pltpu.roll(x, half, axis=1))


def _mla_prep_kernel(cq_ref, ckv_ref, small_ref, cos_ref, sin_ref, qn_ref, kvn_ref,
                     wq_ref, wk_ref, wvt_ref, q_ref, k_ref, vt_ref, *, scale):
    tm = cq_ref.shape[0]
    lane = lax.broadcasted_iota(jnp.int32, (tm, LANES), 1)
    cos = cos_ref[...]
    sin = sin_ref[...]
    cos_q = jnp.where(lane < ROPE_LANE0, 1.0, cos)
    qn = (_rms_scale(cq_ref[...].astype(F32)) * qn_ref[...]).astype(BF16)
    kvn = (_rms_scale(ckv_ref[...].astype(F32)) * kvn_ref[...]).astype(BF16)
    q = jnp.dot(qn, wq_ref[...], preferred_element_type=F32)
    k = jnp.dot(kvn, wk_ref[...], preferred_element_type=F32)
    vt_ref[...] = lax.dot_general(wvt_ref[...], kvn, (((1,), (1,)), ((), ())),
                                  preferred_element_type=F32).astype(BF16)
    kpe = small_ref[...]
    kpe = kpe * cos + _rotate_half(kpe, lane) * sin
    for h in range(MLA_HEADS):
        cols = slice(h * MLA_HEAD_PAD, (h + 1) * MLA_HEAD_PAD)
        qh = q[:, cols]
        qh = qh * cos_q + _rotate_half(qh, lane) * sin
        q_ref[:, cols] = (qh * scale).astype(BF16)
        k_ref[:, cols] = (k[:, cols] + kpe).astype(BF16)


def _pack_mla_weights(w_uq, w_ukv):
    rq = w_uq.shape[0]
    wq = w_uq.reshape(rq, MLA_HEADS, MLA_QK)
    wq = jnp.pad(wq, ((0, 0), (0, 0), (0, MLA_HEAD_PAD - MLA_QK))).reshape(rq, MLA_HEADS * MLA_HEAD_PAD)
    rk = w_ukv.shape[0]
    wkv = w_ukv.reshape(rk, MLA_HEADS, MLA_NOPE + MLA_V)
    wk = jnp.pad(wkv[:, :, :MLA_NOPE], ((0, 0), (0, 0), (0, MLA_HEAD_PAD - MLA_NOPE)))
    wk = wk.reshape(rk, MLA_HEADS * MLA_HEAD_PAD)
    wvt = wkv[:, :, MLA_NOPE:].reshape(rk, MLA_WIDTH).T
    return wq.astype(BF16), wk.astype(BF16), wvt.astype(BF16)


def _mla_prep(cq, ckv, small, cos_t, sin_t, q_norm, kv_norm, wq, wk, wvt, batch, seq):
    t = cq.shape[0]
    tm = _row_tile(seq, 512)
    ns = seq // tm
    qk_w = MLA_HEADS * MLA_HEAD_PAD
    row = lambda n: pl.BlockSpec((tm, n), lambda i: (i, 0))
    return pl.pallas_call(
        functools.partial(_mla_prep_kernel, scale=math.log2(math.e) / math.sqrt(MLA_QK)),
        grid=(t // tm,),
        in_specs=[row(MLA_Q_RANK), row(MLA_KV_RANK), row(LANES), row(LANES), row(LANES),
                  _full((1, MLA_Q_RANK)), _full((1, MLA_KV_RANK)),
                  _full((MLA_Q_RANK, qk_w)), _full((MLA_KV_RANK, qk_w)), _full((MLA_WIDTH, MLA_KV_RANK))],
        out_specs=[row(qk_w), row(qk_w),
                   pl.BlockSpec((None, MLA_WIDTH, tm), lambda i: (i // ns, 0, i % ns))],
        out_shape=[jax.ShapeDtypeStruct((t, qk_w), BF16), jax.ShapeDtypeStruct((t, qk_w), BF16),
                   jax.ShapeDtypeStruct((batch, MLA_WIDTH, seq), BF16)],
        compiler_params=_params("parallel"),
        name="mla_prep",
    )(cq, ckv, small, cos_t, sin_t, q_norm.reshape(1, -1), kv_norm.reshape(1, -1), wq, wk, wvt)


HEADS_PER_STEP = 8


FLASH_ROWS = 32


def _flash_kernel(qi_ref, kj_ref, q_ref, k_ref, vt_ref, o_ref, m_scr, l_scr, acc_scr, s_scr, p_scr):
    p = pl.program_id(2)
    i = qi_ref[p]
    j = kj_ref[p]
    tq, tk = q_ref.shape[0], k_ref.shape[0]

    @pl.when(j == 0)
    def _():
        m_scr[...] = jnp.full_like(m_scr, -1e30)
        l_scr[...] = jnp.zeros_like(l_scr)
        acc_scr[...] = jnp.zeros_like(acc_scr)

    def step(masked):
        m_step = []
        for hh in range(HEADS_PER_STEP):
            q = q_ref[:, hh * MLA_HEAD_PAD:(hh + 1) * MLA_HEAD_PAD]
            k = k_ref[:, hh * MLA_HEAD_PAD:(hh + 1) * MLA_HEAD_PAD]
            s = lax.dot_general(k, q, (((1,), (1,)), ((), ())), preferred_element_type=F32)
            if masked:
                key = lax.broadcasted_iota(jnp.int32, (tk, tq), 0)
                qry = lax.broadcasted_iota(jnp.int32, (tk, tq), 1)
                s = jnp.where(key <= qry, s, -1e30)
            s_scr[hh] = s
            m_step.append(jnp.max(s, axis=0, keepdims=True))
        for hh in range(HEADS_PER_STEP):
            vt = vt_ref[hh * MLA_V:(hh + 1) * MLA_V, :]
            m_prev = m_scr[hh]
            m_new = jnp.maximum(m_prev, m_step[hh])
            alpha = jnp.exp2(m_prev - m_new)
            l8 = jnp.zeros((8, tq), F32)
            for r0 in range(0, tk, FLASH_ROWS):
                pexp = jnp.exp2(s_scr[hh, r0:r0 + FLASH_ROWS, :] - m_new)
                l8 = l8 + jnp.sum(pexp.reshape(FLASH_ROWS // 8, 8, tq), axis=0)
                p_scr[hh, r0:r0 + FLASH_ROWS, :] = pexp.astype(BF16)
            l_scr[hh] = alpha * l_scr[hh] + jnp.sum(l8, axis=0, keepdims=True)
            acc_scr[hh] = alpha * acc_scr[hh] + jnp.dot(vt, p_scr[hh], preferred_element_type=F32)
            m_scr[hh] = m_new

    @pl.when(j < i)
    def _():
        step(False)

    @pl.when(j == i)
    def _():
        step(True)
        o_t = jnp.concatenate([acc_scr[hh] / l_scr[hh] for hh in range(HEADS_PER_STEP)], axis=0)
        o_ref[...] = o_t.T.astype(BF16)


def _flash(q, k, vt, batch, seq):
    tq = _row_tile(seq, 512)
    nq = seq // tq
    pairs = [(i, j) for i in range(nq) for j in range(i + 1)]
    qi = jnp.asarray(np.array([p[0] for p in pairs], np.int32))
    kj = jnp.asarray(np.array([p[1] for p in pairs], np.int32))
    qk_w = HEADS_PER_STEP * MLA_HEAD_PAD
    v_w = HEADS_PER_STEP * MLA_V
    grid_spec = pltpu.PrefetchScalarGridSpec(
        num_scalar_prefetch=2,
        grid=(batch, MLA_HEADS // HEADS_PER_STEP, len(pairs)),
        in_specs=[pl.BlockSpec((tq, qk_w), lambda b, h, p, qi, kj: (b * nq + qi[p], h)),
                  pl.BlockSpec((tq, qk_w), lambda b, h, p, qi, kj: (b * nq + kj[p], h)),
                  pl.BlockSpec((None, v_w, tq), lambda b, h, p, qi, kj: (b, h, kj[p]))],
        out_specs=pl.BlockSpec((tq, v_w), lambda b, h, p, qi, kj: (b * nq + qi[p], h)),
        scratch_shapes=[pltpu.VMEM((HEADS_PER_STEP, 1, tq), F32),
                        pltpu.VMEM((HEADS_PER_STEP, 1, tq), F32),
                        pltpu.VMEM((HEADS_PER_STEP, MLA_V, tq), F32),
                        pltpu.VMEM((HEADS_PER_STEP, tq, tq), F32),
                        pltpu.VMEM((HEADS_PER_STEP, tq, tq), BF16)])
    return pl.pallas_call(
        _flash_kernel,
        grid_spec=grid_spec,
        out_shape=jax.ShapeDtypeStruct((batch * seq, MLA_WIDTH), BF16),
        compiler_params=_params("parallel", "parallel", "arbitrary"),
        name="flash_attn",
    )(qi, kj, q, k, vt)


def _out_proj_kernel(ys_ref, yp_ref, ym_ref, x_ref, w_ref, nw_ref, xo_ref, h_ref):
    a, b = SSD_WIDTH, SSD_WIDTH + POOL_WIDTH
    acc = x_ref[...]
    acc = acc + jnp.dot(ys_ref[...], w_ref[0:a, :], preferred_element_type=F32)
    acc = acc + jnp.dot(yp_ref[...], w_ref[a:b, :], preferred_element_type=F32)
    acc = acc + jnp.dot(ym_ref[...], w_ref[b:, :], preferred_element_type=F32)
    xo_ref[...] = acc
    h_ref[...] = (_rms_scale(acc) * nw_ref[...]).astype(BF16)


def _out_proj(y_ssd, y_pool, y_mla, x, w_out, norm_w):
    t, d = x.shape
    tm = _row_tile(t, 512)
    row = lambda n: pl.BlockSpec((tm, n), lambda i: (i, 0))
    return pl.pallas_call(
        _out_proj_kernel,
        grid=(t // tm,),
        in_specs=[row(SSD_WIDTH), row(POOL_WIDTH), row(MLA_WIDTH), row(d),
                  _full(w_out.shape), _full((1, d))],
        out_specs=[row(d), row(d)],
        out_shape=[jax.ShapeDtypeStruct((t, d), F32), jax.ShapeDtypeStruct((t, d), BF16)],
        compiler_params=_params("parallel"),
        name="out_proj",
    )(y_ssd, y_pool, y_mla, x, w_out.astype(BF16), norm_w.reshape(1, d))


def _ffn_kernel(h_ref, x_ref, wup_ref, cw_ref, cb_ref, wdn_ref, fn_ref, o_ref, tail_scr,
                *, tiles_per_seq, dff, final_norm):
    tm = h_ref.shape[0]

    @pl.when(pl.program_id(0) % tiles_per_seq == 0)
    def _():
        tail_scr[...] = jnp.zeros_like(tail_scr)

    def up_conv(c0):
        cols = slice(c0, c0 + dff)
        u = jnp.dot(h_ref[...], wup_ref[:, cols], preferred_element_type=F32)
        ue = jnp.concatenate([tail_scr[:, cols], u], axis=0)
        tail_scr[:, cols] = u[tm - 8:, :]
        w = cw_ref[:, cols]
        y = cb_ref[:, cols] + w[FFN_CONV - 1:FFN_CONV] * u
        for j in range(1, FFN_CONV):
            y = y + w[FFN_CONV - 1 - j:FFN_CONV - j] * pltpu.roll(ue, j, axis=0)[8:]
        return y

    a = (_silu(up_conv(0)) * up_conv(dff)).astype(BF16)
    acc = x_ref[...] + jnp.dot(a, wdn_ref[...], preferred_element_type=F32)
    if final_norm:
        acc = _rms_scale(acc) * fn_ref[...]
    o_ref[...] = acc


def _ffn(h, x, w_up, conv_w, conv_b, w_down, final_w, seq, final_norm):
    t, d = x.shape
    dff = w_down.shape[0]
    tm = _row_tile(seq, 512)
    assert FFN_CONV - 1 < 8
    row = lambda n: pl.BlockSpec((tm, n), lambda i: (i, 0))
    return pl.pallas_call(
        functools.partial(_ffn_kernel, tiles_per_seq=seq // tm, dff=dff, final_norm=final_norm),
        grid=(t // tm,),
        in_specs=[row(d), row(d), _resident((d, 2 * dff)), _full((FFN_CONV, 2 * dff)), _full((1, 2 * dff)),
                  _resident((dff, d)), _full((1, d))],
        out_specs=row(d),
        out_shape=jax.ShapeDtypeStruct((t, d), F32),
        scratch_shapes=[pltpu.VMEM((8, 2 * dff), F32)],
        compiler_params=_params("arbitrary"),
        name="ffn",
    )(h, x, w_up.astype(BF16), conv_w, conv_b.reshape(1, -1), w_down.astype(BF16), final_w.reshape(1, d))


def kernel(x, positions, attn_norm, w_in, ssd_conv_w, ssd_conv_b, ssd_dt_bias, ssd_a_log, ssd_d, ssd_norm,
           pool_w, pool_scale, mla_q_norm, mla_w_uq, mla_kv_norm, mla_w_ukv, w_out, ffn_norm, ffn_w_up,
           ffn_conv_w, ffn_conv_b, ffn_w_down, final_norm):
    batch, seq, d = x.shape
    depth = w_in.shape[0]
    assert seq % SSD_CHUNK == 0
    cos_t, sin_t = _rope_tables(positions)
    xf = x.reshape(batch * seq, d)
    for l in range(depth):
        z, xs, bc, u, cq, ckv, small = _in_proj(xf, attn_norm[l], _pack_w_in(w_in[l]),
                                                ssd_conv_w[l], ssd_conv_b[l], seq)
        y_ssd = _ssd(xs, bc, z, small, ssd_dt_bias[l], ssd_a_log[l], ssd_d[l], ssd_norm[l], batch, seq)
        y_pool = _pool(u, pool_w[l], pool_scale[l], seq)
        wq, wk, wvt = _pack_mla_weights(mla_w_uq[l], mla_w_ukv[l])
        q, k, vt = _mla_prep(cq, ckv, small, cos_t, sin_t, mla_q_norm[l], mla_kv_norm[l], wq, wk, wvt,
                             batch, seq)
        y_mla = _flash(q, k, vt, batch, seq)
        xf, h = _out_proj(y_ssd, y_pool, y_mla, xf, w_out[l], ffn_norm[l])
        xf = _ffn(h, xf, ffn_w_up[l], ffn_conv_w[l], ffn_conv_b[l], ffn_w_down[l], final_norm, seq,
                  final_norm=(l == depth - 1))
    return xf.reshape(batch, seq, d)
```

```python
import functools
import math

import jax
import jax.numpy as jnp
import numpy as np
from jax import lax
from jax.experimental import pallas as pl
from jax.experimental.pallas import tpu as pltpu

F32 = jnp.float32
BF16 = jnp.bfloat16

EPS = 1e-6
SSD_HEADS = 16
SSD_HEAD_DIM = 64
SSD_WIDTH = SSD_HEADS * SSD_HEAD_DIM
SSD_GROUPS = 2
SSD_HEADS_PER_GROUP = SSD_HEADS // SSD_GROUPS
SSD_STATE = 128
SSD_CONV = 4
SSD_CHUNK = 128
SSD_CONV_CH = SSD_WIDTH + 2 * SSD_GROUPS * SSD_STATE
POOL_GROUPS = 4
POOL_GROUP_DIM = 128
POOL_WIDTH = POOL_GROUPS * POOL_GROUP_DIM
POOL_WINDOWS = (2, 4, 8, 16)
MLA_HEADS = 8
MLA_Q_RANK = 384
MLA_KV_RANK = 256
MLA_NOPE = 64
MLA_ROPE = 32
MLA_V = 64
MLA_QK = MLA_NOPE + MLA_ROPE
MLA_WIDTH = MLA_HEADS * MLA_V
ROPE_THETA = 10000.0
FFN_CONV = 3

LANES = 128
HALO = 16
MLA_HEAD_PAD = LANES
ROPE_LANE0 = MLA_NOPE
DT_LANE0 = 0
VMEM_LIMIT = 56 * 1024 * 1024

_C_Z = 0
_C_XBC = _C_Z + SSD_WIDTH
_C_U = _C_XBC + SSD_CONV_CH
_C_CQ = _C_U + POOL_WIDTH
_C_CKV = _C_CQ + MLA_Q_RANK
_C_SMALL = _C_CKV + MLA_KV_RANK
_C_END = _C_SMALL + LANES


def _silu(x):
    h = 0.5 * x
    return h + h * jnp.tanh(h)


def _rms_scale(x):
    return x * lax.rsqrt(jnp.mean(x * x, axis=-1, keepdims=True) + EPS)


def _row_tile(n, want):
    t = min(n, want)
    assert n % t == 0
    return t


def _params(*sem, flags=None):
    return pltpu.CompilerParams(dimension_semantics=sem, vmem_limit_bytes=VMEM_LIMIT, flags=flags)


def _full(shape):
    zeros = (0,) * len(shape)
    return pl.BlockSpec(shape, lambda *_: zeros)


def _resident(shape):
    zeros = (0,) * len(shape)
    return pl.BlockSpec(shape, lambda *_: zeros, pipeline_mode=pl.Buffered(1))


def _rope_kernel(pos_ref, freq_ref, cos_ref, sin_ref):
    half = MLA_ROPE // 2
    pos = pos_ref[...].astype(F32)
    ang = freq_ref[...] * pos
    c = jnp.cos(ang)
    s = jnp.sin(ang)
    ts = pos.shape[-1]
    lo = jnp.zeros((ROPE_LANE0, ts), F32)
    hi = jnp.zeros((LANES - ROPE_LANE0 - 2 * half, ts), F32)
    cos_ref[...] = jnp.concatenate([lo, c, c, hi], axis=0).T
    sin_ref[...] = jnp.concatenate([lo, -s, s, hi], axis=0).T


def _rope_tables(positions):
    b, s = positions.shape
    ts = _row_tile(s, 512)
    half = MLA_ROPE // 2
    inv_freq = ROPE_THETA ** (-jnp.arange(0, MLA_ROPE, 2, dtype=F32) / MLA_ROPE)
    out = jax.ShapeDtypeStruct((b * s, LANES), F32)
    ns = s // ts
    return pl.pallas_call(
        _rope_kernel,
        grid=(b, ns),
        in_specs=[pl.BlockSpec((None, 1, ts), lambda i, j: (i, 0, j)), _full((half, 1))],
        out_specs=[pl.BlockSpec((ts, LANES), lambda i, j: (i * ns + j, 0))] * 2,
        out_shape=[out, out],
        compiler_params=_params("parallel", "parallel"),
        name="rope_tables",
    )(positions.reshape(b, 1, s), inv_freq.reshape(half, 1))


def _in_proj_kernel(x_ref, nw_ref, w_ref, cw_ref, cb_ref, z_ref, xs_ref, bc_ref, u_ref, cq_ref, ckv_ref,
                    small_ref, tail_scr, *, tiles_per_seq):
    tm = x_ref.shape[0]

    @pl.when(pl.program_id(0) % tiles_per_seq == 0)
    def _():
        tail_scr[...] = jnp.zeros_like(tail_scr)

    h = (_rms_scale(x_ref[...]) * nw_ref[...]).astype(BF16)
    proj = jnp.dot(h, w_ref[...], preferred_element_type=F32)
    z_ref[...] = proj[:, _C_Z:_C_XBC].astype(BF16)
    xbc = proj[:, _C_XBC:_C_U]
    xe = jnp.concatenate([tail_scr[...], xbc], axis=0)
    tail_scr[...] = xbc[tm - 8:, :]
    cw = cw_ref[...]
    conv = cb_ref[...] + cw[SSD_CONV - 1:SSD_CONV] * xbc
    for j in range(1, SSD_CONV):
        conv = conv + cw[SSD_CONV - 1 - j:SSD_CONV - j] * pltpu.roll(xe, j, axis=0)[8:]
    act = _silu(conv).astype(BF16)
    xs_ref[...] = act[:, :SSD_WIDTH]
    bc_ref[...] = act[:, SSD_WIDTH:]
    u_ref[...] = proj[:, _C_U:_C_CQ].astype(BF16)
    cq_ref[...] = proj[:, _C_CQ:_C_CKV].astype(BF16)
    ckv_ref[...] = proj[:, _C_CKV:_C_SMALL].astype(BF16)
    small_ref[...] = proj[:, _C_SMALL:_C_END]


def _pack_w_in(w_in):
    d = w_in.shape[0]
    o = 0
    parts = {}
    for name, n in (("z", SSD_WIDTH), ("xbc", SSD_CONV_CH), ("dt", SSD_HEADS), ("u", POOL_WIDTH),
                    ("cq", MLA_Q_RANK), ("ckv", MLA_KV_RANK), ("kpe", MLA_ROPE)):
        parts[name] = w_in[:, o:o + n]
        o += n
    small = jnp.concatenate([
        parts["dt"], jnp.zeros((d, ROPE_LANE0 - SSD_HEADS), F32),
        parts["kpe"], jnp.zeros((d, LANES - ROPE_LANE0 - MLA_ROPE), F32)], axis=1)
    return jnp.concatenate([parts["z"], parts["xbc"], parts["u"], parts["cq"], parts["ckv"], small],
                           axis=1).astype(BF16)


def _in_proj(x, norm_w, w_packed, conv_w, conv_b, seq):
    t, d = x.shape
    tm = _row_tile(seq, 512)
    assert SSD_CONV - 1 < 8
    bc_w = SSD_CONV_CH - SSD_WIDTH
    widths = (SSD_WIDTH, SSD_WIDTH, bc_w, POOL_WIDTH, MLA_Q_RANK, MLA_KV_RANK)
    row = lambda n: pl.BlockSpec((tm, n), lambda i: (i, 0))
    return pl.pallas_call(
        functools.partial(_in_proj_kernel, tiles_per_seq=seq // tm),
        grid=(t // tm,),
        in_specs=[row(d), _full((1, d)), _full((d, _C_END)),
                  _full((SSD_CONV, SSD_CONV_CH)), _full((1, SSD_CONV_CH))],
        out_specs=[row(n) for n in widths] + [row(LANES)],
        out_shape=[jax.ShapeDtypeStruct((t, n), BF16) for n in widths]
        + [jax.ShapeDtypeStruct((t, LANES), F32)],
        scratch_shapes=[pltpu.VMEM((8, SSD_CONV_CH), F32)],
        compiler_params=_params("arbitrary"),
        name="in_proj",
    )(x, norm_w.reshape(1, d), w_packed, conv_w, conv_b.reshape(1, -1))


SSD_CHUNKS_PER_STEP = 2
SPLIT = 3


def _pack_split(v):
    out = None
    r = v
    for t in range(SPLIT):
        part = r.astype(BF16).astype(F32)
        r = r - part
        placed = part if t == 0 else pltpu.roll(part, t * SSD_HEADS, axis=1)
        out = placed if out is None else out + placed
    return out.astype(BF16)


def _unpack_split(r):
    out = r
    for t in range(1, SPLIT):
        out = out + pltpu.roll(r, LANES - t * SSD_HEADS, axis=1)
    return out


def _ssd_kernel(xs_ref, bc_ref, z_ref, small_ref, dtb_ref, alog_ref, dskip_ref, nw_ref, spread_ref, bcast_ref,
                y_ref, state_scr, yd_scr):
    L, N, P, E = SSD_CHUNK, SSD_STATE, SSD_HEAD_DIM, SSD_HEADS_PER_GROUP
    GW = E * P

    @pl.when(pl.program_id(1) == 0)
    def _():
        state_scr[...] = jnp.zeros_like(state_scr)

    lane = lax.broadcasted_iota(jnp.int32, (L, LANES), 1)
    row = lax.broadcasted_iota(jnp.int32, (L, LANES), 0)
    causal = lane <= row
    head_lane = lane < SSD_HEADS
    tri = jnp.where(causal, 1.0, 0.0).astype(BF16)
    a_neg = -jnp.exp(alog_ref[...])

    for sub in range(SSD_CHUNKS_PER_STEP):
        rows = slice(sub * L, (sub + 1) * L)
        xs_b = xs_ref[rows, :]
        xs = xs_b.astype(F32)
        pre = small_ref[rows, :] + dtb_ref[...]
        softplus = jnp.maximum(pre, 0.0) + jnp.log1p(jnp.exp(-jnp.abs(pre)))
        dt = jnp.where(head_lane, softplus, 0.0)
        da = dt * a_neg
        cum = jnp.where(head_lane, _unpack_split(
            jnp.dot(tri, _pack_split(da), preferred_element_type=F32)), 0.0)
        cum_t = cum.T
        cum_last = cum[L - 1:L, :]
        dte = jnp.where(head_lane, jnp.exp(cum_last - cum), 0.0)
        ecum = jnp.where(head_lane, jnp.exp(cum), 0.0)
        spread = jnp.dot(jnp.concatenate([_pack_split(dt), _pack_split(dte), _pack_split(ecum)], axis=0),
                         spread_ref[...], preferred_element_type=F32)
        dt_x, dte_x, ecum_x = spread[0:L], spread[L:2 * L], spread[2 * L:3 * L]
        cum_b = jnp.dot(_pack_split(cum), bcast_ref[...], preferred_element_type=F32)
        xdt = xs * dt_x
        xdt_b = xdt.astype(BF16)
        xdt_end = (xdt * dte_x).astype(BF16)

        y_off = []
        for g in range(SSD_GROUPS):
            b_g = bc_ref[rows, g * N:(g + 1) * N]
            c_g = bc_ref[rows, SSD_GROUPS * N + g * N:SSD_GROUPS * N + (g + 1) * N]
            cb = lax.dot_general(c_g, b_g, (((1,), (1,)), ((), ())), preferred_element_type=F32)
            state = state_scr[g]
            y_off.append(jnp.dot(c_g, state.astype(BF16), preferred_element_type=F32))
            for e in range(E):
                h = g * E + e
                seg = cum_b[:, h * L:(h + 1) * L] - cum_t[h:h + 1, :]
                decay = jnp.exp(jnp.where(causal, seg, -jnp.inf))
                yd_scr[:, h * P:(h + 1) * P] = jnp.dot((cb * decay).astype(BF16), xdt_b[:, h * P:(h + 1) * P],
                                                       preferred_element_type=F32)
            new = jnp.dot(b_g.astype(F32).T.astype(BF16), xdt_end[:, g * GW:(g + 1) * GW],
                          preferred_element_type=F32)
            state_scr[g] = state * ecum_x[L - 1:L, g * GW:(g + 1) * GW] + new

        y = yd_scr[...] + jnp.concatenate(y_off, axis=1) * ecum_x + xs * dskip_ref[...]
        gated = y * _silu(z_ref[rows, :].astype(F32))
        y_ref[rows, :] = (_rms_scale(gated) * nw_ref[...]).astype(BF16)


def _ssd(xs, bc, z, small, dt_bias, a_log, d_skip, norm_w, batch, seq):
    L = SSD_CHUNK
    rows = L * math.gcd(SSD_CHUNKS_PER_STEP, seq // L)
    assert rows == L * SSD_CHUNKS_PER_STEP and SPLIT * SSD_HEADS <= LANES
    ns = seq // rows
    pad = lambda v: jnp.pad(v, (0, LANES - SSD_HEADS)).reshape(1, LANES)
    k = np.arange(LANES)
    valid = (k < SPLIT * SSD_HEADS)[:, None]
    spread = (valid & ((k % SSD_HEADS)[:, None] == (np.arange(SSD_WIDTH) // SSD_HEAD_DIM)[None, :]))
    bcast = (valid & ((k % SSD_HEADS)[:, None] == (np.arange(SSD_HEADS * L) // L)[None, :]))
    row = lambda n: pl.BlockSpec((rows, n), lambda b, c: (b * ns + c, 0))
    return pl.pallas_call(
        _ssd_kernel,
        grid=(batch, ns),
        in_specs=[row(SSD_WIDTH), row(SSD_CONV_CH - SSD_WIDTH), row(SSD_WIDTH), row(LANES),
                  _full((1, LANES)), _full((1, LANES)), _full((1, SSD_WIDTH)), _full((1, SSD_WIDTH)),
                  _full((LANES, SSD_WIDTH)), _full((LANES, SSD_HEADS * L))],
        out_specs=row(SSD_WIDTH),
        out_shape=jax.ShapeDtypeStruct((batch * seq, SSD_WIDTH), BF16),
        scratch_shapes=[pltpu.VMEM((SSD_GROUPS, SSD_STATE, SSD_HEADS_PER_GROUP * SSD_HEAD_DIM), F32),
                        pltpu.VMEM((L, SSD_WIDTH), F32)],
        compiler_params=_params("parallel", "arbitrary"),
        name="ssd",
    )(xs, bc, z, small, pad(dt_bias), pad(a_log), jnp.repeat(d_skip, SSD_HEAD_DIM).reshape(1, -1),
      norm_w.reshape(1, -1), jnp.asarray(spread, BF16), jnp.asarray(bcast, BF16))


def _pool_tile(u_ref, halo_ref, pw_ref, ps_ref, start):
    tm = u_ref.shape[0]
    x = u_ref[...].astype(F32)
    halo = jnp.where(start == 0, 0.0, halo_ref[...].astype(F32))
    xe = jnp.concatenate([halo, x], axis=0)
    pos = start + lax.broadcasted_iota(jnp.int32, (tm, 1), 0)
    out = []
    for gi, w in enumerate(POOL_WINDOWS):
        cols = slice(gi * POOL_GROUP_DIM, (gi + 1) * POOL_GROUP_DIM)
        acc = xe[:, cols]
        sh = 1
        while sh < w:
            acc = acc + pltpu.roll(acc, sh, axis=0)
            sh *= 2
        cnt = jnp.minimum(pos + 1, w).astype(F32)
        pooled = acc[HALO:] / cnt - x[:, cols]
        yg = jnp.dot(pooled.astype(BF16), pw_ref[gi], preferred_element_type=F32)
        out.append((yg * ps_ref[:, cols]).astype(BF16))
    return jnp.concatenate(out, axis=1)


def _rotate_half(x, lane):
    half = MLA_ROPE // 2
    return jnp.where(lane < ROPE_LANE0 + half,
                     pltpu.roll(x, LANES - half, axis=1), pltpu.roll(x, half, axis=1))


def _mla_prep_kernel(cq_ref, ckv_ref, small_ref, cos_ref, sin_ref, qn_ref, kvn_ref,
                     wq_ref, wk_ref, wvt_ref, q_ref, k_ref, vt_ref, *, scale):
    tm = cq_ref.shape[0]
    lane = lax.broadcasted_iota(jnp.int32, (tm, LANES), 1)
    cos = cos_ref[...]
    sin = sin_ref[...]
    cos_q = jnp.where(lane < ROPE_LANE0, 1.0, cos)
    qn = (_rms_scale(cq_ref[...].astype(F32)) * qn_ref[...]).astype(BF16)
    kvn = (_rms_scale(ckv_ref[...].astype(F32)) * kvn_ref[...]).astype(BF16)
    q = jnp.dot(qn, wq_ref[...], preferred_element_type=F32)
    k = jnp.dot(kvn, wk_ref[...], preferred_element_type=F32)
    vt_ref[...] = lax.dot_general(wvt_ref[...], kvn, (((1,), (1,)), ((), ())),
                                  preferred_element_type=F32).astype(BF16)
    kpe = small_ref[...]
    kpe = kpe * cos + _rotate_half(kpe, lane) * sin
    for h in range(MLA_HEADS):
        cols = slice(h * MLA_HEAD_PAD, (h + 1) * MLA_HEAD_PAD)
        qh = q[:, cols]
        qh = qh * cos_q + _rotate_half(qh, lane) * sin
        q_ref[:, cols] = (qh * scale).astype(BF16)
        k_ref[:, cols] = (k[:, cols] + kpe).astype(BF16)


def _pack_mla_weights(w_uq, w_ukv):
    rq = w_uq.shape[0]
    wq = w_uq.reshape(rq, MLA_HEADS, MLA_QK)
    wq = jnp.pad(wq, ((0, 0), (0, 0), (0, MLA_HEAD_PAD - MLA_QK))).reshape(rq, MLA_HEADS * MLA_HEAD_PAD)
    rk = w_ukv.shape[0]
    wkv = w_ukv.reshape(rk, MLA_HEADS, MLA_NOPE + MLA_V)
    wk = jnp.pad(wkv[:, :, :MLA_NOPE], ((0, 0), (0, 0), (0, MLA_HEAD_PAD - MLA_NOPE)))
    wk = wk.reshape(rk, MLA_HEADS * MLA_HEAD_PAD)
    wvt = wkv[:, :, MLA_NOPE:].reshape(rk, MLA_WIDTH).T
    return wq.astype(BF16), wk.astype(BF16), wvt.astype(BF16)


def _mla_prep(cq, ckv, small, cos_t, sin_t, q_norm, kv_norm, wq, wk, wvt, batch, seq):
    t = cq.shape[0]
    tm = _row_tile(seq, 512)
    ns = seq // tm
    qk_w = MLA_HEADS * MLA_HEAD_PAD
    row = lambda n: pl.BlockSpec((tm, n), lambda i: (i, 0))
    return pl.pallas_call(
        functools.partial(_mla_prep_kernel, scale=math.log2(math.e) / math.sqrt(MLA_QK)),
        grid=(t // tm,),
        in_specs=[row(MLA_Q_RANK), row(MLA_KV_RANK), row(LANES), row(LANES), row(LANES),
                  _full((1, MLA_Q_RANK)), _full((1, MLA_KV_RANK)),
                  _full((MLA_Q_RANK, qk_w)), _full((MLA_KV_RANK, qk_w)), _full((MLA_WIDTH, MLA_KV_RANK))],
        out_specs=[row(qk_w), row(qk_w),
                   pl.BlockSpec((None, MLA_WIDTH, tm), lambda i: (i // ns, 0, i % ns))],
        out_shape=[jax.ShapeDtypeStruct((t, qk_w), BF16), jax.ShapeDtypeStruct((t, qk_w), BF16),
                   jax.ShapeDtypeStruct((batch, MLA_WIDTH, seq), BF16)],
        compiler_params=_params("parallel"),
        name="mla_prep",
    )(cq, ckv, small, cos_t, sin_t, q_norm.reshape(1, -1), kv_norm.reshape(1, -1), wq, wk, wvt)


HEADS_PER_STEP = 8


PV_TILE = 256
V_AUG = MLA_V + 16


def _flash_kernel(qi_ref, kj_ref, q_ref, k_ref, vt_ref, o_ref, m_scr, acc_scr, s_scr):
    p = pl.program_id(2)
    i = qi_ref[p]
    j = kj_ref[p]
    tq, tk = q_ref.shape[0], k_ref.shape[0]

    @pl.when(j == 0)
    def _():
        m_scr[...] = jnp.full_like(m_scr, -1e30)
        acc_scr[...] = jnp.zeros_like(acc_scr)

    def step(masked):
        m_step = []
        for hh in range(HEADS_PER_STEP):
            q = q_ref[:, hh * MLA_HEAD_PAD:(hh + 1) * MLA_HEAD_PAD]
            k = k_ref[:, hh * MLA_HEAD_PAD:(hh + 1) * MLA_HEAD_PAD]
            s = lax.dot_general(k, q, (((1,), (1,)), ((), ())), preferred_element_type=F32)
            if masked:
                key = lax.broadcasted_iota(jnp.int32, (tk, tq), 0)
                qry = lax.broadcasted_iota(jnp.int32, (tk, tq), 1)
                s = jnp.where(key <= qry, s, -1e30)
            s_scr[hh] = s
            m_step.append(jnp.max(s, axis=0, keepdims=True))
        ones = jnp.ones((V_AUG - MLA_V, tk), BF16)
        for hh in range(HEADS_PER_STEP):
            vt = jnp.concatenate([vt_ref[hh * MLA_V:(hh + 1) * MLA_V, :], ones], axis=0)
            m_prev = m_scr[hh]
            m_new = jnp.maximum(m_prev, m_step[hh])
            alpha = jnp.exp2(m_prev - m_new)
            parts = []
            for n0 in range(0, tq, PV_TILE):
                upd = None
                for k0 in range(0, tk, PV_TILE):
                    pexp = jnp.exp2(s_scr[hh, k0:k0 + PV_TILE, n0:n0 + PV_TILE] - m_new[:, n0:n0 + PV_TILE])
                    d = jnp.dot(vt[:, k0:k0 + PV_TILE], pexp.astype(BF16), preferred_element_type=F32)
                    upd = d if upd is None else upd + d
                parts.append(upd)
            acc_scr[hh] = alpha * acc_scr[hh] + jnp.concatenate(parts, axis=1)
            m_scr[hh] = m_new

    @pl.when(j < i)
    def _():
        step(False)

    @pl.when(j == i)
    def _():
        step(True)
        o_t = jnp.concatenate([acc_scr[hh, :MLA_V] / acc_scr[hh, MLA_V:MLA_V + 1]
                               for hh in range(HEADS_PER_STEP)], axis=0)
        o_ref[...] = o_t.T.astype(BF16)


def _flash(q, k, vt, batch, seq):
    tq = _row_tile(seq, 512)
    nq = seq // tq
    pairs = [(i, j) for i in range(nq) for j in range(i + 1)]
    qi = jnp.asarray(np.array([p[0] for p in pairs], np.int32))
    kj = jnp.asarray(np.array([p[1] for p in pairs], np.int32))
    qk_w = HEADS_PER_STEP * MLA_HEAD_PAD
    v_w = HEADS_PER_STEP * MLA_V
    grid_spec = pltpu.PrefetchScalarGridSpec(
        num_scalar_prefetch=2,
        grid=(batch, MLA_HEADS // HEADS_PER_STEP, len(pairs)),
        in_specs=[pl.BlockSpec((tq, qk_w), lambda b, h, p, qi, kj: (b * nq + qi[p], h)),
                  pl.BlockSpec((tq, qk_w), lambda b, h, p, qi, kj: (b * nq + kj[p], h)),
                  pl.BlockSpec((None, v_w, tq), lambda b, h, p, qi, kj: (b, h, kj[p]))],
        out_specs=pl.BlockSpec((tq, v_w), lambda b, h, p, qi, kj: (b * nq + qi[p], h)),
        scratch_shapes=[pltpu.VMEM((HEADS_PER_STEP, 1, tq), F32),
                        pltpu.VMEM((HEADS_PER_STEP, V_AUG, tq), F32),
                        pltpu.VMEM((HEADS_PER_STEP, tq, tq), F32)])
    return pl.pallas_call(
        _flash_kernel,
        grid_spec=grid_spec,
        out_shape=jax.ShapeDtypeStruct((batch * seq, MLA_WIDTH), BF16),
        compiler_params=_params("parallel", "parallel", "arbitrary"),
        name="flash_attn",
    )(qi, kj, q, k, vt)


def _out_proj_kernel(ys_ref, u_ref, halo_ref, ym_ref, x_ref, w_ref, pw_ref, ps_ref, nw_ref, xo_ref, h_ref,
                     *, seq):
    a, b = SSD_WIDTH, SSD_WIDTH + POOL_WIDTH
    tm = x_ref.shape[0]
    y_pool = _pool_tile(u_ref, halo_ref, pw_ref, ps_ref, (pl.program_id(0) * tm) % seq)
    acc = x_ref[...]
    acc = acc + jnp.dot(ys_ref[...], w_ref[0:a, :], preferred_element_type=F32)
    acc = acc + jnp.dot(y_pool, w_ref[a:b, :], preferred_element_type=F32)
    acc = acc + jnp.dot(ym_ref[...], w_ref[b:, :], preferred_element_type=F32)
    xo_ref[...] = acc
    h_ref[...] = (_rms_scale(acc) * nw_ref[...]).astype(BF16)


def _out_proj(y_ssd, u, y_mla, x, w_out, pool_w, pool_scale, norm_w, seq):
    t, d = x.shape
    tm = _row_tile(seq, 512)
    assert max(POOL_WINDOWS) - 1 < HALO and tm % HALO == 0
    hb = tm // HALO
    row = lambda n: pl.BlockSpec((tm, n), lambda i: (i, 0))
    return pl.pallas_call(
        functools.partial(_out_proj_kernel, seq=seq),
        grid=(t // tm,),
        in_specs=[row(SSD_WIDTH), row(POOL_WIDTH),
                  pl.BlockSpec((HALO, POOL_WIDTH), lambda i: (jnp.maximum(i * hb - 1, 0), 0)),
                  row(MLA_WIDTH), row(d), _full(w_out.shape),
                  _full((POOL_GROUPS, POOL_GROUP_DIM, POOL_GROUP_DIM)), _full((1, POOL_WIDTH)), _full((1, d))],
        out_specs=[row(d), row(d)],
        out_shape=[jax.ShapeDtypeStruct((t, d), F32), jax.ShapeDtypeStruct((t, d), BF16)],
        compiler_params=_params("parallel"),
        name="out_proj",
    )(y_ssd, u, u, y_mla, x, w_out.astype(BF16), pool_w.astype(BF16), pool_scale.reshape(1, -1),
      norm_w.reshape(1, d))


def _ffn_kernel(h_ref, x_ref, wup_ref, cw_ref, cb_ref, wdn_ref, fn_ref, o_ref, tail_scr,
                *, tiles_per_seq, dff, final_norm):
    tm = h_ref.shape[0]

    @pl.when(pl.program_id(0) % tiles_per_seq == 0)
    def _():
        tail_scr[...] = jnp.zeros_like(tail_scr)

    def up_conv(c0):
        cols = slice(c0, c0 + dff)
        u = jnp.dot(h_ref[...], wup_ref[:, cols], preferred_element_type=F32)
        ue = jnp.concatenate([tail_scr[:, cols], u], axis=0)
        tail_scr[:, cols] = u[tm - 8:, :]
        w = cw_ref[:, cols]
        y = cb_ref[:, cols] + w[FFN_CONV - 1:FFN_CONV] * u
        for j in range(1, FFN_CONV):
            y = y + w[FFN_CONV - 1 - j:FFN_CONV - j] * pltpu.roll(ue, j, axis=0)[8:]
        return y

    a = (_silu(up_conv(0)) * up_conv(dff)).astype(BF16)
    acc = x_ref[...] + jnp.dot(a, wdn_ref[...], preferred_element_type=F32)
    if final_norm:
        acc = _rms_scale(acc) * fn_ref[...]
    o_ref[...] = acc


def _ffn(h, x, w_up, conv_w, conv_b, w_down, final_w, seq, final_norm):
    t, d = x.shape
    dff = w_down.shape[0]
    tm = _row_tile(seq, 512)
    assert FFN_CONV - 1 < 8
    row = lambda n: pl.BlockSpec((tm, n), lambda i: (i, 0))
    return pl.pallas_call(
        functools.partial(_ffn_kernel, tiles_per_seq=seq // tm, dff=dff, final_norm=final_norm),
        grid=(t // tm,),
        in_specs=[row(d), row(d), _resident((d, 2 * dff)), _full((FFN_CONV, 2 * dff)), _full((1, 2 * dff)),
                  _resident((dff, d)), _full((1, d))],
        out_specs=row(d),
        out_shape=jax.ShapeDtypeStruct((t, d), F32),
        scratch_shapes=[pltpu.VMEM((8, 2 * dff), F32)],
        compiler_params=_params("arbitrary"),
        name="ffn",
    )(h, x, w_up.astype(BF16), conv_w, conv_b.reshape(1, -1), w_down.astype(BF16), final_w.reshape(1, d))


def kernel(x, positions, attn_norm, w_in, ssd_conv_w, ssd_conv_b, ssd_dt_bias, ssd_a_log, ssd_d, ssd_norm,
           pool_w, pool_scale, mla_q_norm, mla_w_uq, mla_kv_norm, mla_w_ukv, w_out, ffn_norm, ffn_w_up,
           ffn_conv_w, ffn_conv_b, ffn_w_down, final_norm):
    batch, seq, d = x.shape
    depth = w_in.shape[0]
    assert seq % SSD_CHUNK == 0
    cos_t, sin_t = _rope_tables(positions)
    xf = x.reshape(batch * seq, d)
    for l in range(depth):
        z, xs, bc, u, cq, ckv, small = _in_proj(xf, attn_norm[l], _pack_w_in(w_in[l]),
                                                ssd_conv_w[l], ssd_conv_b[l], seq)
        y_ssd = _ssd(xs, bc, z, small, ssd_dt_bias[l], ssd_a_log[l], ssd_d[l], ssd_norm[l], batch, seq)
        wq, wk, wvt = _pack_mla_weights(mla_w_uq[l], mla_w_ukv[l])
        q, k, vt = _mla_prep(cq, ckv, small, cos_t, sin_t, mla_q_norm[l], mla_kv_norm[l], wq, wk, wvt,
                             batch, seq)
        y_mla = _flash(q, k, vt, batch, seq)
        xf, h = _out_proj(y_ssd, u, y_mla, xf, w_out[l], pool_w[l], pool_scale[l], ffn_norm[l], seq)
        xf = _ffn(h, xf, ffn_w_up[l], ffn_conv_w[l], ffn_conv_b[l], ffn_w_down[l], final_norm, seq,
                  final_norm=(l == depth - 1))
    return xf.reshape(batch, seq, d)
```

```python
import functools
import math

import jax
import jax.numpy as jnp
import numpy as np
from jax import lax
from jax.experimental import pallas as pl
from jax.experimental.pallas import tpu as pltpu

F32 = jnp.float32
BF16 = jnp.bfloat16

EPS = 1e-6
SSD_HEADS = 16
SSD_HEAD_DIM = 64
SSD_WIDTH = SSD_HEADS * SSD_HEAD_DIM
SSD_GROUPS = 2
SSD_HEADS_PER_GROUP = SSD_HEADS // SSD_GROUPS
SSD_STATE = 128
SSD_CONV = 4
SSD_CHUNK = 128
SSD_CONV_CH = SSD_WIDTH + 2 * SSD_GROUPS * SSD_STATE
POOL_GROUPS = 4
POOL_GROUP_DIM = 128
POOL_WIDTH = POOL_GROUPS * POOL_GROUP_DIM
POOL_WINDOWS = (2, 4, 8, 16)
MLA_HEADS = 8
MLA_Q_RANK = 384
MLA_KV_RANK = 256
MLA_NOPE = 64
MLA_ROPE = 32
MLA_V = 64
MLA_QK = MLA_NOPE + MLA_ROPE
MLA_WIDTH = MLA_HEADS * MLA_V
ROPE_THETA = 10000.0
FFN_CONV = 3

LANES = 128
HALO = 16
MLA_HEAD_PAD = LANES
ROPE_HALF = MLA_ROPE // 2
ROPE_X1 = 32
ROPE_X2 = ROPE_X1 + LANES // 2
VMEM_LIMIT = 56 * 1024 * 1024

_C_Z = 0
_C_XBC = _C_Z + SSD_WIDTH
_C_U = _C_XBC + SSD_CONV_CH
_C_CQ = _C_U + POOL_WIDTH
_C_CKV = _C_CQ + MLA_Q_RANK
_C_SMALL = _C_CKV + MLA_KV_RANK
_C_END = _C_SMALL + LANES


def _silu(x):
    h = 0.5 * x
    return h + h * jnp.tanh(h)


def _rms_scale(x):
    return x * lax.rsqrt(jnp.mean(x * x, axis=-1, keepdims=True) + EPS)


def _row_tile(n, want):
    t = min(n, want)
    assert n % t == 0
    return t


def _params(*sem, flags=None):
    return pltpu.CompilerParams(dimension_semantics=sem, vmem_limit_bytes=VMEM_LIMIT, flags=flags)


def _full(shape):
    zeros = (0,) * len(shape)
    return pl.BlockSpec(shape, lambda *_: zeros)


def _resident(shape):
    zeros = (0,) * len(shape)
    return pl.BlockSpec(shape, lambda *_: zeros, pipeline_mode=pl.Buffered(1))


def _rope_kernel(pos_ref, freq_ref, cos_ref, sin_ref):
    pos = pos_ref[...].astype(F32)
    ang = freq_ref[...] * pos
    c = jnp.cos(ang)
    s = jnp.sin(ang)
    ts = pos.shape[-1]
    lo = jnp.zeros((ROPE_X1, ts), F32)
    mid = jnp.zeros((ROPE_X2 - ROPE_X1 - ROPE_HALF, ts), F32)
    hi = jnp.zeros((LANES - ROPE_X2 - ROPE_HALF, ts), F32)
    cos_ref[...] = jnp.concatenate([lo, c, mid, c, hi], axis=0).T
    sin_ref[...] = jnp.concatenate([lo, -s, mid, s, hi], axis=0).T


def _rope_tables(positions):
    b, s = positions.shape
    ts = _row_tile(s, 512)
    half = MLA_ROPE // 2
    inv_freq = ROPE_THETA ** (-jnp.arange(0, MLA_ROPE, 2, dtype=F32) / MLA_ROPE)
    out = jax.ShapeDtypeStruct((b * s, LANES), F32)
    ns = s // ts
    return pl.pallas_call(
        _rope_kernel,
        grid=(b, ns),
        in_specs=[pl.BlockSpec((None, 1, ts), lambda i, j: (i, 0, j)), _full((half, 1))],
        out_specs=[pl.BlockSpec((ts, LANES), lambda i, j: (i * ns + j, 0))] * 2,
        out_shape=[out, out],
        compiler_params=_params("parallel", "parallel"),
        name="rope_tables",
    )(positions.reshape(b, 1, s), inv_freq.reshape(half, 1))


def _in_proj_kernel(x_ref, nw_ref, w_ref, cw_ref, cb_ref, z_ref, xs_ref, bc_ref, u_ref, cq_ref, ckv_ref,
                    small_ref, tail_scr, *, tiles_per_seq):
    tm = x_ref.shape[0]

    @pl.when(pl.program_id(0) % tiles_per_seq == 0)
    def _():
        tail_scr[...] = jnp.zeros_like(tail_scr)

    h = (_rms_scale(x_ref[...]) * nw_ref[...]).astype(BF16)
    proj = jnp.dot(h, w_ref[...], preferred_element_type=F32)
    z_ref[...] = proj[:, _C_Z:_C_XBC].astype(BF16)
    xbc = proj[:, _C_XBC:_C_U]
    xe = jnp.concatenate([tail_scr[...], xbc], axis=0)
    tail_scr[...] = xbc[tm - 8:, :]
    cw = cw_ref[...]
    conv = cb_ref[...] + cw[SSD_CONV - 1:SSD_CONV] * xbc
    for j in range(1, SSD_CONV):
        conv = conv + cw[SSD_CONV - 1 - j:SSD_CONV - j] * pltpu.roll(xe, j, axis=0)[8:]
    act = _silu(conv).astype(BF16)
    xs_ref[...] = act[:, :SSD_WIDTH]
    bc_ref[...] = act[:, SSD_WIDTH:]
    u_ref[...] = proj[:, _C_U:_C_CQ].astype(BF16)
    cq_ref[...] = proj[:, _C_CQ:_C_CKV].astype(BF16)
    ckv_ref[...] = proj[:, _C_CKV:_C_SMALL].astype(BF16)
    small_ref[...] = proj[:, _C_SMALL:_C_END]


def _pack_w_in(w_in):
    d = w_in.shape[0]
    o = 0
    parts = {}
    for name, n in (("z", SSD_WIDTH), ("xbc", SSD_CONV_CH), ("dt", SSD_HEADS), ("u", POOL_WIDTH),
                    ("cq", MLA_Q_RANK), ("ckv", MLA_KV_RANK), ("kpe", MLA_ROPE)):
        parts[name] = w_in[:, o:o + n]
        o += n
    small = jnp.concatenate([
        parts["dt"], jnp.zeros((d, ROPE_X1 - SSD_HEADS), F32),
        parts["kpe"][:, :ROPE_HALF], jnp.zeros((d, ROPE_X2 - ROPE_X1 - ROPE_HALF), F32),
        parts["kpe"][:, ROPE_HALF:], jnp.zeros((d, LANES - ROPE_X2 - ROPE_HALF), F32)], axis=1)
    return jnp.concatenate([parts["z"], parts["xbc"], parts["u"], parts["cq"], parts["ckv"], small],
                           axis=1).astype(BF16)


def _in_proj(x, norm_w, w_packed, conv_w, conv_b, seq):
    t, d = x.shape
    tm = _row_tile(seq, 512)
    assert SSD_CONV - 1 < 8
    bc_w = SSD_CONV_CH - SSD_WIDTH
    widths = (SSD_WIDTH, SSD_WIDTH, bc_w, POOL_WIDTH, MLA_Q_RANK, MLA_KV_RANK)
    row = lambda n: pl.BlockSpec((tm, n), lambda i: (i, 0))
    return pl.pallas_call(
        functools.partial(_in_proj_kernel, tiles_per_seq=seq // tm),
        grid=(t // tm,),
        in_specs=[row(d), _full((1, d)), _full((d, _C_END)),
                  _full((SSD_CONV, SSD_CONV_CH)), _full((1, SSD_CONV_CH))],
        out_specs=[row(n) for n in widths] + [row(LANES)],
        out_shape=[jax.ShapeDtypeStruct((t, n), BF16) for n in widths]
        + [jax.ShapeDtypeStruct((t, LANES), F32)],
        scratch_shapes=[pltpu.VMEM((8, SSD_CONV_CH), F32)],
        compiler_params=_params("arbitrary"),
        name="in_proj",
    )(x, norm_w.reshape(1, d), w_packed, conv_w, conv_b.reshape(1, -1))


SSD_CHUNKS_PER_STEP = 2
SPLIT = 3


def _pack_split(v):
    out = None
    r = v
    for t in range(SPLIT):
        part = r.astype(BF16).astype(F32)
        r = r - part
        placed = part if t == 0 else pltpu.roll(part, t * SSD_HEADS, axis=1)
        out = placed if out is None else out + placed
    return out.astype(BF16)


def _unpack_split(r):
    out = r
    for t in range(1, SPLIT):
        out = out + pltpu.roll(r, LANES - t * SSD_HEADS, axis=1)
    return out


def _ssd_kernel(xs_ref, bc_ref, z_ref, small_ref, dtb_ref, alog_ref, dskip_ref, nw_ref, spread_ref, bcast_ref,
                y_ref, state_scr):
    L, N, P, E = SSD_CHUNK, SSD_STATE, SSD_HEAD_DIM, SSD_HEADS_PER_GROUP
    GW = E * P

    @pl.when(pl.program_id(1) == 0)
    def _():
        state_scr[...] = jnp.zeros_like(state_scr)

    lane = lax.broadcasted_iota(jnp.int32, (L, LANES), 1)
    row = lax.broadcasted_iota(jnp.int32, (L, LANES), 0)
    causal = lane <= row
    head_lane = lane < SSD_HEADS
    tri = jnp.where(causal, 1.0, 0.0).astype(BF16)
    a_neg = -jnp.exp(alog_ref[...])

    for sub in range(SSD_CHUNKS_PER_STEP):
        rows = slice(sub * L, (sub + 1) * L)
        xs_b = xs_ref[rows, :]
        xs = xs_b.astype(F32)
        pre = small_ref[rows, :] + dtb_ref[...]
        softplus = jnp.maximum(pre, 0.0) + jnp.log1p(jnp.exp(-jnp.abs(pre)))
        dt = jnp.where(head_lane, softplus, 0.0)
        da = dt * a_neg
        cum = jnp.where(head_lane, _unpack_split(
            jnp.dot(tri, _pack_split(da), preferred_element_type=F32)), 0.0)
        cum_t = cum.T
        cum_last = cum[L - 1:L, :]
        dte = jnp.where(head_lane, jnp.exp(cum_last - cum), 0.0)
        ecum = jnp.where(head_lane, jnp.exp(cum), 0.0)
        spread = jnp.dot(jnp.concatenate([_pack_split(dt), _pack_split(dte), _pack_split(ecum)], axis=0),
                         spread_ref[...], preferred_element_type=F32)
        dt_x, dte_x, ecum_x = spread[0:L], spread[L:2 * L], spread[2 * L:3 * L]
        cum_b = jnp.dot(_pack_split(cum), bcast_ref[...], preferred_element_type=F32)
        xdt = xs * dt_x
        xdt_b = xdt.astype(BF16)
        xdt_end = (xdt * dte_x).astype(BF16)

        y_off, y_diag = [], []
        for g in range(SSD_GROUPS):
            b_g = bc_ref[rows, g * N:(g + 1) * N]
            c_g = bc_ref[rows, SSD_GROUPS * N + g * N:SSD_GROUPS * N + (g + 1) * N]
            cb = lax.dot_general(c_g, b_g, (((1,), (1,)), ((), ())), preferred_element_type=F32)
            state = state_scr[g]
            y_off.append(jnp.dot(c_g, state.astype(BF16), preferred_element_type=F32))
            for e in range(0, E, 2):
                h = g * E + e
                ms = []
                for hh in (h, h + 1):
                    seg = cum_b[:, hh * L:(hh + 1) * L] - cum_t[hh:hh + 1, :]
                    ms.append((cb * jnp.exp(jnp.where(causal, seg, -jnp.inf))).astype(BF16))
                x2 = xdt_b[:, h * P:(h + 2) * P]
                zero = jnp.zeros_like(x2)
                rhs = jnp.concatenate([jnp.where(lane < P, x2, zero), jnp.where(lane < P, zero, x2)], axis=0)
                y_diag.append(jnp.dot(jnp.concatenate(ms, axis=1), rhs, preferred_element_type=F32))
            new = jnp.dot(b_g.astype(F32).T.astype(BF16), xdt_end[:, g * GW:(g + 1) * GW],
                          preferred_element_type=F32)
            state_scr[g] = state * ecum_x[L - 1:L, g * GW:(g + 1) * GW] + new

        y = jnp.concatenate(y_diag, axis=1) + jnp.concatenate(y_off, axis=1) * ecum_x + xs * dskip_ref[...]
        gated = y * _silu(z_ref[rows, :].astype(F32))
        y_ref[rows, :] = (_rms_scale(gated) * nw_ref[...]).astype(BF16)


def _ssd(xs, bc, z, small, dt_bias, a_log, d_skip, norm_w, batch, seq):
    L = SSD_CHUNK
    rows = L * math.gcd(SSD_CHUNKS_PER_STEP, seq // L)
    assert rows == L * SSD_CHUNKS_PER_STEP and SPLIT * SSD_HEADS <= LANES
    ns = seq // rows
    pad = lambda v: jnp.pad(v, (0, LANES - SSD_HEADS)).reshape(1, LANES)
    k = np.arange(LANES)
    valid = (k < SPLIT * SSD_HEADS)[:, None]
    spread = (valid & ((k % SSD_HEADS)[:, None] == (np.arange(SSD_WIDTH) // SSD_HEAD_DIM)[None, :]))
    bcast = (valid & ((k % SSD_HEADS)[:, None] == (np.arange(SSD_HEADS * L) // L)[None, :]))
    row = lambda n: pl.BlockSpec((rows, n), lambda b, c: (b * ns + c, 0))
    return pl.pallas_call(
        _ssd_kernel,
        grid=(batch, ns),
        in_specs=[row(SSD_WIDTH), row(SSD_CONV_CH - SSD_WIDTH), row(SSD_WIDTH), row(LANES),
                  _full((1, LANES)), _full((1, LANES)), _full((1, SSD_WIDTH)), _full((1, SSD_WIDTH)),
                  _full((LANES, SSD_WIDTH)), _full((LANES, SSD_HEADS * L))],
        out_specs=row(SSD_WIDTH),
        out_shape=jax.ShapeDtypeStruct((batch * seq, SSD_WIDTH), BF16),
        scratch_shapes=[pltpu.VMEM((SSD_GROUPS, SSD_STATE, SSD_HEADS_PER_GROUP * SSD_HEAD_DIM), F32)],
        compiler_params=_params("parallel", "arbitrary"),
        name="ssd",
    )(xs, bc, z, small, pad(dt_bias), pad(a_log), jnp.repeat(d_skip, SSD_HEAD_DIM).reshape(1, -1),
      norm_w.reshape(1, -1), jnp.asarray(spread, BF16), jnp.asarray(bcast, BF16))


def _pool_tile(u_ref, halo_ref, pw_ref, ps_ref, start):
    tm = u_ref.shape[0]
    x = u_ref[...].astype(F32)
    halo = jnp.where(start == 0, 0.0, halo_ref[...].astype(F32))
    xe = jnp.concatenate([halo, x], axis=0)
    pos = start + lax.broadcasted_iota(jnp.int32, (tm, 1), 0)
    out = []
    for gi, w in enumerate(POOL_WINDOWS):
        cols = slice(gi * POOL_GROUP_DIM, (gi + 1) * POOL_GROUP_DIM)
        acc = xe[:, cols]
        sh = 1
        while sh < w:
            acc = acc + pltpu.roll(acc, sh, axis=0)
            sh *= 2
        cnt = jnp.minimum(pos + 1, w).astype(F32)
        pooled = acc[HALO:] / cnt - x[:, cols]
        yg = jnp.dot(pooled.astype(BF16), pw_ref[gi], preferred_element_type=F32)
        out.append((yg * ps_ref[:, cols]).astype(BF16))
    return jnp.concatenate(out, axis=1)


def _rotate_half(x):
    return pltpu.roll(x, LANES // 2, axis=1)


def _mla_prep_kernel(cq_ref, ckv_ref, small_ref, cos_ref, sin_ref, qn_ref, kvn_ref,
                     wq_ref, wk_ref, wvt_ref, q_ref, k_ref, vt_ref, *, scale):
    tm = cq_ref.shape[0]
    lane = lax.broadcasted_iota(jnp.int32, (tm, LANES), 1)
    cos = cos_ref[...]
    sin = sin_ref[...]
    rope_lane = ((lane >= ROPE_X1) & (lane < ROPE_X1 + ROPE_HALF)) | ((lane >= ROPE_X2) & (lane < ROPE_X2 + ROPE_HALF))
    cos_q = jnp.where(rope_lane, cos, 1.0)
    qn = (_rms_scale(cq_ref[...].astype(F32)) * qn_ref[...]).astype(BF16)
    kvn = (_rms_scale(ckv_ref[...].astype(F32)) * kvn_ref[...]).astype(BF16)
    q = jnp.dot(qn, wq_ref[...], preferred_element_type=F32)
    k = jnp.dot(kvn, wk_ref[...], preferred_element_type=F32)
    vt_ref[...] = lax.dot_general(wvt_ref[...], kvn, (((1,), (1,)), ((), ())),
                                  preferred_element_type=F32).astype(BF16)
    kpe = small_ref[...]
    kpe = kpe * cos + _rotate_half(kpe) * sin
    for h in range(MLA_HEADS):
        cols = slice(h * MLA_HEAD_PAD, (h + 1) * MLA_HEAD_PAD)
        qh = q[:, cols]
        qh = qh * cos_q + _rotate_half(qh) * sin
        q_ref[:, cols] = (qh * scale).astype(BF16)
        k_ref[:, cols] = (k[:, cols] + kpe).astype(BF16)


def _head_layout(nope, x1, x2):
    r, h, _ = nope.shape
    gap = jnp.zeros((r, h, LANES // 2 - ROPE_X1 - ROPE_HALF), nope.dtype)
    if x1 is None:
        x1 = x2 = jnp.zeros((r, h, ROPE_HALF), nope.dtype)
    half = MLA_NOPE // 2
    return jnp.concatenate([nope[:, :, :half], x1, gap, nope[:, :, half:], x2, gap], axis=2).reshape(r, h * LANES)


def _pack_mla_weights(w_uq, w_ukv):
    assert ROPE_X1 == MLA_NOPE // 2
    rq = w_uq.shape[0]
    wq = w_uq.reshape(rq, MLA_HEADS, MLA_QK)
    wq = _head_layout(wq[:, :, :MLA_NOPE], wq[:, :, MLA_NOPE:MLA_NOPE + ROPE_HALF], wq[:, :, MLA_NOPE + ROPE_HALF:])
    rk = w_ukv.shape[0]
    wkv = w_ukv.reshape(rk, MLA_HEADS, MLA_NOPE + MLA_V)
    wk = _head_layout(wkv[:, :, :MLA_NOPE], None, None)
    wvt = wkv[:, :, MLA_NOPE:].reshape(rk, MLA_WIDTH).T
    return wq.astype(BF16), wk.astype(BF16), wvt.astype(BF16)


def _mla_prep(cq, ckv, small, cos_t, sin_t, q_norm, kv_norm, wq, wk, wvt, batch, seq):
    t = cq.shape[0]
    tm = _row_tile(seq, 512)
    ns = seq // tm
    qk_w = MLA_HEADS * MLA_HEAD_PAD
    row = lambda n: pl.BlockSpec((tm, n), lambda i: (i, 0))
    return pl.pallas_call(
        functools.partial(_mla_prep_kernel, scale=math.log2(math.e) / math.sqrt(MLA_QK)),
        grid=(t // tm,),
        in_specs=[row(MLA_Q_RANK), row(MLA_KV_RANK), row(LANES), row(LANES), row(LANES),
                  _full((1, MLA_Q_RANK)), _full((1, MLA_KV_RANK)),
                  _full((MLA_Q_RANK, qk_w)), _full((MLA_KV_RANK, qk_w)), _full((MLA_WIDTH, MLA_KV_RANK))],
        out_specs=[row(qk_w), row(qk_w),
                   pl.BlockSpec((None, MLA_WIDTH, tm), lambda i: (i // ns, 0, i % ns))],
        out_shape=[jax.ShapeDtypeStruct((t, qk_w), BF16), jax.ShapeDtypeStruct((t, qk_w), BF16),
                   jax.ShapeDtypeStruct((batch, MLA_WIDTH, seq), BF16)],
        compiler_params=_params("parallel"),
        name="mla_prep",
    )(cq, ckv, small, cos_t, sin_t, q_norm.reshape(1, -1), kv_norm.reshape(1, -1), wq, wk, wvt)


HEADS_PER_STEP = 8


PV_TILE = 256
V_AUG = MLA_V + 16


def _flash_kernel(qi_ref, kj_ref, q_ref, k_ref, vt_ref, o_ref, m_scr, acc_scr, s_scr):
    p = pl.program_id(2)
    i = qi_ref[p]
    j = kj_ref[p]
    tq, tk = q_ref.shape[0], k_ref.shape[0]

    @pl.when(j == 0)
    def _():
        m_scr[...] = jnp.full_like(m_scr, -1e30)
        acc_scr[...] = jnp.zeros_like(acc_scr)

    def step(masked):
        m_step = []
        for hh in range(HEADS_PER_STEP):
            q = q_ref[:, hh * MLA_HEAD_PAD:(hh + 1) * MLA_HEAD_PAD]
            k = k_ref[:, hh * MLA_HEAD_PAD:(hh + 1) * MLA_HEAD_PAD]
            s = lax.dot_general(k, q, (((1,), (1,)), ((), ())), preferred_element_type=F32)
            if masked:
                key = lax.broadcasted_iota(jnp.int32, (tk, tq), 0)
                qry = lax.broadcasted_iota(jnp.int32, (tk, tq), 1)
                s = jnp.where(key <= qry, s, -1e30)
            s_scr[hh] = s
            m_step.append(jnp.max(s, axis=0, keepdims=True))
        ones = jnp.ones((V_AUG - MLA_V, tk), BF16)
        for hh in range(HEADS_PER_STEP):
            vt = jnp.concatenate([vt_ref[hh * MLA_V:(hh + 1) * MLA_V, :], ones], axis=0)
            m_prev = m_scr[hh]
            m_new = jnp.maximum(m_prev, m_step[hh])
            alpha = jnp.exp2(m_prev - m_new)
            parts = []
            for n0 in range(0, tq, PV_TILE):
                upd = None
                for k0 in range(0, tk, PV_TILE):
                    pexp = jnp.exp2(s_scr[hh, k0:k0 + PV_TILE, n0:n0 + PV_TILE] - m_new[:, n0:n0 + PV_TILE])
                    d = jnp.dot(vt[:, k0:k0 + PV_TILE], pexp.astype(BF16), preferred_element_type=F32)
                    upd = d if upd is None else upd + d
                parts.append(upd)
            acc_scr[hh] = alpha * acc_scr[hh] + jnp.concatenate(parts, axis=1)
            m_scr[hh] = m_new

    @pl.when(j < i)
    def _():
        step(False)

    @pl.when(j == i)
    def _():
        step(True)
        o_t = jnp.concatenate([acc_scr[hh, :MLA_V] / acc_scr[hh, MLA_V:MLA_V + 1]
                               for hh in range(HEADS_PER_STEP)], axis=0)
        o_ref[...] = o_t.T.astype(BF16)


def _flash(q, k, vt, batch, seq):
    tq = _row_tile(seq, 512)
    nq = seq // tq
    pairs = [(i, j) for i in range(nq) for j in range(i + 1)]
    qi = jnp.asarray(np.array([p[0] for p in pairs], np.int32))
    kj = jnp.asarray(np.array([p[1] for p in pairs], np.int32))
    qk_w = HEADS_PER_STEP * MLA_HEAD_PAD
    v_w = HEADS_PER_STEP * MLA_V
    grid_spec = pltpu.PrefetchScalarGridSpec(
        num_scalar_prefetch=2,
        grid=(batch, MLA_HEADS // HEADS_PER_STEP, len(pairs)),
        in_specs=[pl.BlockSpec((tq, qk_w), lambda b, h, p, qi, kj: (b * nq + qi[p], h)),
                  pl.BlockSpec((tq, qk_w), lambda b, h, p, qi, kj: (b * nq + kj[p], h)),
                  pl.BlockSpec((None, v_w, tq), lambda b, h, p, qi, kj: (b, h, kj[p]))],
        out_specs=pl.BlockSpec((tq, v_w), lambda b, h, p, qi, kj: (b * nq + qi[p], h)),
        scratch_shapes=[pltpu.VMEM((HEADS_PER_STEP, 1, tq), F32),
                        pltpu.VMEM((HEADS_PER_STEP, V_AUG, tq), F32),
                        pltpu.VMEM((HEADS_PER_STEP, tq, tq), F32)])
    return pl.pallas_call(
        _flash_kernel,
        grid_spec=grid_spec,
        out_shape=jax.ShapeDtypeStruct((batch * seq, MLA_WIDTH), BF16),
        compiler_params=_params("parallel", "parallel", "arbitrary"),
        name="flash_attn",
    )(qi, kj, q, k, vt)


def _out_proj_kernel(ys_ref, u_ref, halo_ref, ym_ref, x_ref, w_ref, pw_ref, ps_ref, nw_ref, xo_ref, h_ref,
                     *, seq):
    a, b = SSD_WIDTH, SSD_WIDTH + POOL_WIDTH
    tm = x_ref.shape[0]
    y_pool = _pool_tile(u_ref, halo_ref, pw_ref, ps_ref, (pl.program_id(0) * tm) % seq)
    acc = x_ref[...]
    acc = acc + jnp.dot(ys_ref[...], w_ref[0:a, :], preferred_element_type=F32)
    acc = acc + jnp.dot(y_pool, w_ref[a:b, :], preferred_element_type=F32)
    acc = acc + jnp.dot(ym_ref[...], w_ref[b:, :], preferred_element_type=F32)
    xo_ref[...] = acc
    h_ref[...] = (_rms_scale(acc) * nw_ref[...]).astype(BF16)


def _out_proj(y_ssd, u, y_mla, x, w_out, pool_w, pool_scale, norm_w, seq):
    t, d = x.shape
    tm = _row_tile(seq, 512)
    assert max(POOL_WINDOWS) - 1 < HALO and tm % HALO == 0
    hb = tm // HALO
    row = lambda n: pl.BlockSpec((tm, n), lambda i: (i, 0))
    return pl.pallas_call(
        functools.partial(_out_proj_kernel, seq=seq),
        grid=(t // tm,),
        in_specs=[row(SSD_WIDTH), row(POOL_WIDTH),
                  pl.BlockSpec((HALO, POOL_WIDTH), lambda i: (jnp.maximum(i * hb - 1, 0), 0)),
                  row(MLA_WIDTH), row(d), _full(w_out.shape),
                  _full((POOL_GROUPS, POOL_GROUP_DIM, POOL_GROUP_DIM)), _full((1, POOL_WIDTH)), _full((1, d))],
        out_specs=[row(d), row(d)],
        out_shape=[jax.ShapeDtypeStruct((t, d), F32), jax.ShapeDtypeStruct((t, d), BF16)],
        compiler_params=_params("parallel"),
        name="out_proj",
    )(y_ssd, u, u, y_mla, x, w_out.astype(BF16), pool_w.astype(BF16), pool_scale.reshape(1, -1),
      norm_w.reshape(1, d))


def _ffn_kernel(h_ref, x_ref, wup_ref, cw_ref, cb_ref, wdn_ref, fn_ref, o_ref, tail_scr,
                *, tiles_per_seq, dff, final_norm):
    tm = h_ref.shape[0]

    @pl.when(pl.program_id(0) % tiles_per_seq == 0)
    def _():
        tail_scr[...] = jnp.zeros_like(tail_scr)

    def up_conv(c0):
        cols = slice(c0, c0 + dff)
        u = jnp.dot(h_ref[...], wup_ref[:, cols], preferred_element_type=F32)
        ue = jnp.concatenate([tail_scr[:, cols], u], axis=0)
        tail_scr[:, cols] = u[tm - 8:, :]
        w = cw_ref[:, cols]
        y = cb_ref[:, cols] + w[FFN_CONV - 1:FFN_CONV] * u
        for j in range(1, FFN_CONV):
            y = y + w[FFN_CONV - 1 - j:FFN_CONV - j] * pltpu.roll(ue, j, axis=0)[8:]
        return y

    a = (_silu(up_conv(0)) * up_conv(dff)).astype(BF16)
    acc = x_ref[...] + jnp.dot(a, wdn_ref[...], preferred_element_type=F32)
    if final_norm:
        acc = _rms_scale(acc) * fn_ref[...]
    o_ref[...] = acc


def _ffn(h, x, w_up, conv_w, conv_b, w_down, final_w, seq, final_norm):
    t, d = x.shape
    dff = w_down.shape[0]
    tm = _row_tile(seq, 512)
    assert FFN_CONV - 1 < 8
    row = lambda n: pl.BlockSpec((tm, n), lambda i: (i, 0))
    return pl.pallas_call(
        functools.partial(_ffn_kernel, tiles_per_seq=seq // tm, dff=dff, final_norm=final_norm),
        grid=(t // tm,),
        in_specs=[row(d), row(d), _resident((d, 2 * dff)), _full((FFN_CONV, 2 * dff)), _full((1, 2 * dff)),
                  _resident((dff, d)), _full((1, d))],
        out_specs=row(d),
        out_shape=jax.ShapeDtypeStruct((t, d), F32),
        scratch_shapes=[pltpu.VMEM((8, 2 * dff), F32)],
        compiler_params=_params("arbitrary"),
        name="ffn",
    )(h, x, w_up.astype(BF16), conv_w, conv_b.reshape(1, -1), w_down.astype(BF16), final_w.reshape(1, d))


def kernel(x, positions, attn_norm, w_in, ssd_conv_w, ssd_conv_b, ssd_dt_bias, ssd_a_log, ssd_d, ssd_norm,
           pool_w, pool_scale, mla_q_norm, mla_w_uq, mla_kv_norm, mla_w_ukv, w_out, ffn_norm, ffn_w_up,
           ffn_conv_w, ffn_conv_b, ffn_w_down, final_norm):
    batch, seq, d = x.shape
    depth = w_in.shape[0]
    assert seq % SSD_CHUNK == 0
    cos_t, sin_t = _rope_tables(positions)
    xf = x.reshape(batch * seq, d)
    for l in range(depth):
        z, xs, bc, u, cq, ckv, small = _in_proj(xf, attn_norm[l], _pack_w_in(w_in[l]),
                                                ssd_conv_w[l], ssd_conv_b[l], seq)
        y_ssd = _ssd(xs, bc, z, small, ssd_dt_bias[l], ssd_a_log[l], ssd_d[l], ssd_norm[l], batch, seq)
        wq, wk, wvt = _pack_mla_weights(mla_w_uq[l], mla_w_ukv[l])
        q, k, vt = _mla_prep(cq, ckv, small, cos_t, sin_t, mla_q_norm[l], mla_kv_norm[l], wq, wk, wvt,
                             batch, seq)
        y_mla = _flash(q, k, vt, batch, seq)
        xf, h = _out_proj(y_ssd, u, y_mla, xf, w_out[l], pool_w[l], pool_scale[l], ffn_norm[l], seq)
        xf = _ffn(h, xf, ffn_w_up[l], ffn_conv_w[l], ffn_conv_b[l], ffn_w_down[l], final_norm, seq,
                  final_norm=(l == depth - 1))
    return xf.reshape(batch, seq, d)
```

```python
import functools
import math

import jax
import jax.numpy as jnp
import numpy as np
from jax import lax
from jax.experimental import pallas as pl
from jax.experimental.pallas import tpu as pltpu

F32 = jnp.float32
BF16 = jnp.bfloat16

EPS = 1e-6
SSD_HEADS = 16
SSD_HEAD_DIM = 64
SSD_WIDTH = SSD_HEADS * SSD_HEAD_DIM
SSD_GROUPS = 2
SSD_HEADS_PER_GROUP = SSD_HEADS // SSD_GROUPS
SSD_STATE = 128
SSD_CONV = 4
SSD_CHUNK = 128
SSD_CONV_CH = SSD_WIDTH + 2 * SSD_GROUPS * SSD_STATE
POOL_GROUPS = 4
POOL_GROUP_DIM = 128
POOL_WIDTH = POOL_GROUPS * POOL_GROUP_DIM
POOL_WINDOWS = (2, 4, 8, 16)
MLA_HEADS = 8
MLA_Q_RANK = 384
MLA_KV_RANK = 256
MLA_NOPE = 64
MLA_ROPE = 32
MLA_V = 64
MLA_QK = MLA_NOPE + MLA_ROPE
MLA_WIDTH = MLA_HEADS * MLA_V
ROPE_THETA = 10000.0
FFN_CONV = 3

LANES = 128
HALO = 16
MLA_HEAD_PAD = LANES
ROPE_HALF = MLA_ROPE // 2
ROPE_X1 = 32
ROPE_X2 = ROPE_X1 + LANES // 2
VMEM_LIMIT = 56 * 1024 * 1024

_C_Z = 0
_C_XBC = _C_Z + SSD_WIDTH
_C_U = _C_XBC + SSD_CONV_CH
_C_CQ = _C_U + POOL_WIDTH
_C_CKV = _C_CQ + MLA_Q_RANK
_C_SMALL = _C_CKV + MLA_KV_RANK
_C_END = _C_SMALL + LANES


def _silu(x):
    h = 0.5 * x
    return h + h * jnp.tanh(h)


def _rms_scale(x):
    return x * lax.rsqrt(jnp.mean(x * x, axis=-1, keepdims=True) + EPS)


def _row_tile(n, want):
    t = min(n, want)
    assert n % t == 0
    return t


def _params(*sem, flags=None):
    return pltpu.CompilerParams(dimension_semantics=sem, vmem_limit_bytes=VMEM_LIMIT, flags=flags)


def _full(shape):
    zeros = (0,) * len(shape)
    return pl.BlockSpec(shape, lambda *_: zeros)


def _resident(shape):
    zeros = (0,) * len(shape)
    return pl.BlockSpec(shape, lambda *_: zeros, pipeline_mode=pl.Buffered(1))


def _rope_kernel(pos_ref, freq_ref, cos_ref, sin_ref):
    pos = pos_ref[...].astype(F32)
    ang = freq_ref[...] * pos
    c = jnp.cos(ang)
    s = jnp.sin(ang)
    ts = pos.shape[-1]
    lo = jnp.zeros((ROPE_X1, ts), F32)
    mid = jnp.zeros((ROPE_X2 - ROPE_X1 - ROPE_HALF, ts), F32)
    hi = jnp.zeros((LANES - ROPE_X2 - ROPE_HALF, ts), F32)
    cos_ref[...] = jnp.concatenate([lo, c, mid, c, hi], axis=0).T
    sin_ref[...] = jnp.concatenate([lo, -s, mid, s, hi], axis=0).T


def _rope_tables(positions):
    b, s = positions.shape
    ts = _row_tile(s, 4096)
    half = MLA_ROPE // 2
    inv_freq = ROPE_THETA ** (-jnp.arange(0, MLA_ROPE, 2, dtype=F32) / MLA_ROPE)
    out = jax.ShapeDtypeStruct((b * s, LANES), F32)
    ns = s // ts
    return pl.pallas_call(
        _rope_kernel,
        grid=(b, ns),
        in_specs=[pl.BlockSpec((None, 1, ts), lambda i, j: (i, 0, j)), _full((half, 1))],
        out_specs=[pl.BlockSpec((ts, LANES), lambda i, j: (i * ns + j, 0))] * 2,
        out_shape=[out, out],
        compiler_params=_params("parallel", "parallel"),
        name="rope_tables",
    )(positions.reshape(b, 1, s), inv_freq.reshape(half, 1))


def _in_proj_kernel(x_ref, nw_ref, w_ref, cw_ref, cb_ref, z_ref, xs_ref, bc_ref, u_ref, cq_ref, ckv_ref,
                    small_ref, tail_scr, *, tiles_per_seq):
    tm = x_ref.shape[0]

    @pl.when(pl.program_id(0) % tiles_per_seq == 0)
    def _():
        tail_scr[...] = jnp.zeros_like(tail_scr)

    h = (_rms_scale(x_ref[...]) * nw_ref[...]).astype(BF16)
    proj = jnp.dot(h, w_ref[...], preferred_element_type=F32)
    z_ref[...] = proj[:, _C_Z:_C_XBC].astype(BF16)
    xbc = proj[:, _C_XBC:_C_U]
    xe = jnp.concatenate([tail_scr[...], xbc], axis=0)
    tail_scr[...] = xbc[tm - 8:, :]
    cw = cw_ref[...]
    conv = cb_ref[...] + cw[SSD_CONV - 1:SSD_CONV] * xbc
    for j in range(1, SSD_CONV):
        conv = conv + cw[SSD_CONV - 1 - j:SSD_CONV - j] * pltpu.roll(xe, j, axis=0)[8:]
    act = _silu(conv).astype(BF16)
    xs_ref[...] = act[:, :SSD_WIDTH]
    bc_ref[...] = act[:, SSD_WIDTH:]
    u_ref[...] = proj[:, _C_U:_C_CQ].astype(BF16)
    cq_ref[...] = proj[:, _C_CQ:_C_CKV].astype(BF16)
    ckv_ref[...] = proj[:, _C_CKV:_C_SMALL].astype(BF16)
    small_ref[...] = proj[:, _C_SMALL:_C_END]


def _pack_w_in(w_in):
    d = w_in.shape[0]
    o = 0
    parts = {}
    for name, n in (("z", SSD_WIDTH), ("xbc", SSD_CONV_CH), ("dt", SSD_HEADS), ("u", POOL_WIDTH),
                    ("cq", MLA_Q_RANK), ("ckv", MLA_KV_RANK), ("kpe", MLA_ROPE)):
        parts[name] = w_in[:, o:o + n]
        o += n
    small = jnp.concatenate([
        parts["dt"], jnp.zeros((d, ROPE_X1 - SSD_HEADS), F32),
        parts["kpe"][:, :ROPE_HALF], jnp.zeros((d, ROPE_X2 - ROPE_X1 - ROPE_HALF), F32),
        parts["kpe"][:, ROPE_HALF:], jnp.zeros((d, LANES - ROPE_X2 - ROPE_HALF), F32)], axis=1)
    return jnp.concatenate([parts["z"], parts["xbc"], parts["u"], parts["cq"], parts["ckv"], small],
                           axis=1).astype(BF16)


def _in_proj(x, norm_w, w_packed, conv_w, conv_b, seq):
    t, d = x.shape
    tm = _row_tile(seq, 512)
    assert SSD_CONV - 1 < 8
    bc_w = SSD_CONV_CH - SSD_WIDTH
    widths = (SSD_WIDTH, SSD_WIDTH, bc_w, POOL_WIDTH, MLA_Q_RANK, MLA_KV_RANK)
    row = lambda n: pl.BlockSpec((tm, n), lambda i: (i, 0))
    return pl.pallas_call(
        functools.partial(_in_proj_kernel, tiles_per_seq=seq // tm),
        grid=(t // tm,),
        in_specs=[row(d), _full((1, d)), _full((d, _C_END)),
                  _full((SSD_CONV, SSD_CONV_CH)), _full((1, SSD_CONV_CH))],
        out_specs=[row(n) for n in widths] + [row(LANES)],
        out_shape=[jax.ShapeDtypeStruct((t, n), BF16) for n in widths]
        + [jax.ShapeDtypeStruct((t, LANES), F32)],
        scratch_shapes=[pltpu.VMEM((8, SSD_CONV_CH), F32)],
        compiler_params=_params("arbitrary"),
        name="in_proj",
    )(x, norm_w.reshape(1, d), w_packed, conv_w, conv_b.reshape(1, -1))


SSD_CHUNKS_PER_STEP = 2
SPLIT = 3


def _pack_split(v):
    out = None
    r = v
    for t in range(SPLIT):
        part = r.astype(BF16).astype(F32)
        r = r - part
        placed = part if t == 0 else pltpu.roll(part, t * SSD_HEADS, axis=1)
        out = placed if out is None else out + placed
    return out.astype(BF16)


def _unpack_split(r):
    out = r
    for t in range(1, SPLIT):
        out = out + pltpu.roll(r, LANES - t * SSD_HEADS, axis=1)
    return out


def _ssd_kernel(xs_ref, bc_ref, z_ref, small_ref, dtb_ref, alog_ref, dskip_ref, nw_ref, spread_ref, bcast_ref,
                y_ref, state_scr):
    L, N, P, E = SSD_CHUNK, SSD_STATE, SSD_HEAD_DIM, SSD_HEADS_PER_GROUP
    GW = E * P

    @pl.when(pl.program_id(1) == 0)
    def _():
        state_scr[...] = jnp.zeros_like(state_scr)

    lane = lax.broadcasted_iota(jnp.int32, (L, LANES), 1)
    row = lax.broadcasted_iota(jnp.int32, (L, LANES), 0)
    causal = lane <= row
    head_lane = lane < SSD_HEADS
    tri = jnp.where(causal, 1.0, 0.0).astype(BF16)
    a_neg = -jnp.exp(alog_ref[...])

    for sub in range(SSD_CHUNKS_PER_STEP):
        rows = slice(sub * L, (sub + 1) * L)
        xs_b = xs_ref[rows, :]
        xs = xs_b.astype(F32)
        pre = small_ref[rows, :] + dtb_ref[...]
        softplus = jnp.maximum(pre, 0.0) + jnp.log1p(jnp.exp(-jnp.abs(pre)))
        dt = jnp.where(head_lane, softplus, 0.0)
        da = dt * a_neg
        cum = jnp.where(head_lane, _unpack_split(
            jnp.dot(tri, _pack_split(da), preferred_element_type=F32)), 0.0)
        cum_t = cum.T
        cum_last = cum[L - 1:L, :]
        dte = jnp.where(head_lane, jnp.exp(cum_last - cum), 0.0)
        ecum = jnp.where(head_lane, jnp.exp(cum), 0.0)
        spread = jnp.dot(jnp.concatenate([_pack_split(dt), _pack_split(dte), _pack_split(ecum)], axis=0),
                         spread_ref[...], preferred_element_type=F32)
        dt_x, dte_x, ecum_x = spread[0:L], spread[L:2 * L], spread[2 * L:3 * L]
        cum_b = jnp.dot(_pack_split(cum), bcast_ref[...], preferred_element_type=F32)
        xdt = xs * dt_x
        xdt_b = xdt.astype(BF16)
        xdt_end = (xdt * dte_x).astype(BF16)

        y_off, y_diag = [], []
        for g in range(SSD_GROUPS):
            b_g = bc_ref[rows, g * N:(g + 1) * N]
            c_g = bc_ref[rows, SSD_GROUPS * N + g * N:SSD_GROUPS * N + (g + 1) * N]
            cb = lax.dot_general(c_g, b_g, (((1,), (1,)), ((), ())), preferred_element_type=F32)
            state = state_scr[g]
            y_off.append(jnp.dot(c_g, state.astype(BF16), preferred_element_type=F32))
            for e in range(0, E, 2):
                h = g * E + e
                ms = []
                for hh in (h, h + 1):
                    seg = cum_b[:, hh * L:(hh + 1) * L] - cum_t[hh:hh + 1, :]
                    ms.append((cb * jnp.exp(jnp.where(causal, seg, -jnp.inf))).astype(BF16))
                x2 = xdt_b[:, h * P:(h + 2) * P]
                zero = jnp.zeros_like(x2)
                rhs = jnp.concatenate([jnp.where(lane < P, x2, zero), jnp.where(lane < P, zero, x2)], axis=0)
                y_diag.append(jnp.dot(jnp.concatenate(ms, axis=1), rhs, preferred_element_type=F32))
            new = jnp.dot(b_g.astype(F32).T.astype(BF16), xdt_end[:, g * GW:(g + 1) * GW],
                          preferred_element_type=F32)
            state_scr[g] = state * ecum_x[L - 1:L, g * GW:(g + 1) * GW] + new

        y = jnp.concatenate(y_diag, axis=1) + jnp.concatenate(y_off, axis=1) * ecum_x + xs * dskip_ref[...]
        gated = y * _silu(z_ref[rows, :].astype(F32))
        y_ref[rows, :] = (_rms_scale(gated) * nw_ref[...]).astype(BF16)


def _ssd(xs, bc, z, small, dt_bias, a_log, d_skip, norm_w, batch, seq):
    L = SSD_CHUNK
    rows = L * math.gcd(SSD_CHUNKS_PER_STEP, seq // L)
    assert rows == L * SSD_CHUNKS_PER_STEP and SPLIT * SSD_HEADS <= LANES
    ns = seq // rows
    pad = lambda v: jnp.pad(v, (0, LANES - SSD_HEADS)).reshape(1, LANES)
    k = np.arange(LANES)
    valid = (k < SPLIT * SSD_HEADS)[:, None]
    spread = (valid & ((k % SSD_HEADS)[:, None] == (np.arange(SSD_WIDTH) // SSD_HEAD_DIM)[None, :]))
    bcast = (valid & ((k % SSD_HEADS)[:, None] == (np.arange(SSD_HEADS * L) // L)[None, :]))
    row = lambda n: pl.BlockSpec((rows, n), lambda b, c: (b * ns + c, 0))
    return pl.pallas_call(
        _ssd_kernel,
        grid=(batch, ns),
        in_specs=[row(SSD_WIDTH), row(SSD_CONV_CH - SSD_WIDTH), row(SSD_WIDTH), row(LANES),
                  _full((1, LANES)), _full((1, LANES)), _full((1, SSD_WIDTH)), _full((1, SSD_WIDTH)),
                  _full((LANES, SSD_WIDTH)), _full((LANES, SSD_HEADS * L))],
        out_specs=row(SSD_WIDTH),
        out_shape=jax.ShapeDtypeStruct((batch * seq, SSD_WIDTH), BF16),
        scratch_shapes=[pltpu.VMEM((SSD_GROUPS, SSD_STATE, SSD_HEADS_PER_GROUP * SSD_HEAD_DIM), F32)],
        compiler_params=_params("parallel", "arbitrary"),
        name="ssd",
    )(xs, bc, z, small, pad(dt_bias), pad(a_log), jnp.repeat(d_skip, SSD_HEAD_DIM).reshape(1, -1),
      norm_w.reshape(1, -1), jnp.asarray(spread, BF16), jnp.asarray(bcast, BF16))


def _pool_tile(u_ref, halo_ref, pw_ref, ps_ref, start):
    tm = u_ref.shape[0]
    x = u_ref[...].astype(F32)
    halo = jnp.where(start == 0, 0.0, halo_ref[...].astype(F32))
    xe = jnp.concatenate([halo, x], axis=0)
    pos = start + lax.broadcasted_iota(jnp.int32, (tm, 1), 0)
    out = []
    for gi, w in enumerate(POOL_WINDOWS):
        cols = slice(gi * POOL_GROUP_DIM, (gi + 1) * POOL_GROUP_DIM)
        acc = xe[:, cols]
        sh = 1
        while sh < w:
            acc = acc + pltpu.roll(acc, sh, axis=0)
            sh *= 2
        cnt = jnp.minimum(pos + 1, w).astype(F32)
        pooled = acc[HALO:] / cnt - x[:, cols]
        yg = jnp.dot(pooled.astype(BF16), pw_ref[gi], preferred_element_type=F32)
        out.append((yg * ps_ref[:, cols]).astype(BF16))
    return jnp.concatenate(out, axis=1)


def _rotate_half(x):
    return pltpu.roll(x, LANES // 2, axis=1)


def _mla_prep_kernel(cq_ref, ckv_ref, small_ref, cos_ref, sin_ref, qn_ref, kvn_ref,
                     wq_ref, wk_ref, wvt_ref, q_ref, k_ref, vt_ref, *, scale):
    tm = cq_ref.shape[0]
    lane = lax.broadcasted_iota(jnp.int32, (tm, LANES), 1)
    cos = cos_ref[...]
    sin = sin_ref[...]
    rope_lane = ((lane >= ROPE_X1) & (lane < ROPE_X1 + ROPE_HALF)) | ((lane >= ROPE_X2) & (lane < ROPE_X2 + ROPE_HALF))
    cos_q = jnp.where(rope_lane, cos, 1.0)
    qn = (_rms_scale(cq_ref[...].astype(F32)) * qn_ref[...]).astype(BF16)
    kvn = (_rms_scale(ckv_ref[...].astype(F32)) * kvn_ref[...]).astype(BF16)
    q = jnp.dot(qn, wq_ref[...], preferred_element_type=F32)
    k = jnp.dot(kvn, wk_ref[...], preferred_element_type=F32)
    vt_ref[...] = lax.dot_general(wvt_ref[...], kvn, (((1,), (1,)), ((), ())),
                                  preferred_element_type=F32).astype(BF16)
    kpe = small_ref[...]
    kpe = kpe * cos + _rotate_half(kpe) * sin
    for h in range(MLA_HEADS):
        cols = slice(h * MLA_HEAD_PAD, (h + 1) * MLA_HEAD_PAD)
        qh = q[:, cols]
        qh = qh * cos_q + _rotate_half(qh) * sin
        q_ref[:, cols] = (qh * scale).astype(BF16)
        k_ref[:, cols] = (k[:, cols] + kpe).astype(BF16)


def _head_layout(nope, x1, x2):
    r, h, _ = nope.shape
    gap = jnp.zeros((r, h, LANES // 2 - ROPE_X1 - ROPE_HALF), nope.dtype)
    if x1 is None:
        x1 = x2 = jnp.zeros((r, h, ROPE_HALF), nope.dtype)
    half = MLA_NOPE // 2
    return jnp.concatenate([nope[:, :, :half], x1, gap, nope[:, :, half:], x2, gap], axis=2).reshape(r, h * LANES)


def _pack_mla_weights(w_uq, w_ukv):
    assert ROPE_X1 == MLA_NOPE // 2
    rq = w_uq.shape[0]
    wq = w_uq.reshape(rq, MLA_HEADS, MLA_QK)
    wq = _head_layout(wq[:, :, :MLA_NOPE], wq[:, :, MLA_NOPE:MLA_NOPE + ROPE_HALF], wq[:, :, MLA_NOPE + ROPE_HALF:])
    rk = w_ukv.shape[0]
    wkv = w_ukv.reshape(rk, MLA_HEADS, MLA_NOPE + MLA_V)
    wk = _head_layout(wkv[:, :, :MLA_NOPE], None, None)
    wvt = wkv[:, :, MLA_NOPE:].reshape(rk, MLA_WIDTH).T
    return wq.astype(BF16), wk.astype(BF16), wvt.astype(BF16)


def _mla_prep(cq, ckv, small, cos_t, sin_t, q_norm, kv_norm, wq, wk, wvt, batch, seq):
    t = cq.shape[0]
    tm = _row_tile(seq, 512)
    ns = seq // tm
    qk_w = MLA_HEADS * MLA_HEAD_PAD
    row = lambda n: pl.BlockSpec((tm, n), lambda i: (i, 0))
    return pl.pallas_call(
        functools.partial(_mla_prep_kernel, scale=math.log2(math.e) / math.sqrt(MLA_QK)),
        grid=(t // tm,),
        in_specs=[row(MLA_Q_RANK), row(MLA_KV_RANK), row(LANES), row(LANES), row(LANES),
                  _full((1, MLA_Q_RANK)), _full((1, MLA_KV_RANK)),
                  _full((MLA_Q_RANK, qk_w)), _full((MLA_KV_RANK, qk_w)), _full((MLA_WIDTH, MLA_KV_RANK))],
        out_specs=[row(qk_w), row(qk_w),
                   pl.BlockSpec((None, MLA_WIDTH, tm), lambda i: (i // ns, 0, i % ns))],
        out_shape=[jax.ShapeDtypeStruct((t, qk_w), BF16), jax.ShapeDtypeStruct((t, qk_w), BF16),
                   jax.ShapeDtypeStruct((batch, MLA_WIDTH, seq), BF16)],
        compiler_params=_params("parallel"),
        name="mla_prep",
    )(cq, ckv, small, cos_t, sin_t, q_norm.reshape(1, -1), kv_norm.reshape(1, -1), wq, wk, wvt)


HEADS_PER_STEP = 8


LOOKAHEAD = 6
PV_TILE = 256
V_AUG = MLA_V + 16


def _flash_kernel(qi_ref, kj_ref, q_ref, k_ref, vt_ref, o_ref, m_scr, acc_scr, s_scr):
    p = pl.program_id(2)
    i = qi_ref[p]
    j = kj_ref[p]
    tq, tk = q_ref.shape[0], k_ref.shape[0]

    @pl.when(j == 0)
    def _():
        m_scr[...] = jnp.full_like(m_scr, -1e30)
        acc_scr[...] = jnp.zeros_like(acc_scr)

    def step(masked):
        ones = jnp.ones((V_AUG - MLA_V, tk), BF16)
        half = tq // 2
        m_half = {}

        def visible(c):
            return min(tk, (c + 1) * half) if masked else tk

        def scores(hh, c):
            n0 = c * half
            kr = visible(c)
            q = q_ref[n0:n0 + half, hh * MLA_HEAD_PAD:(hh + 1) * MLA_HEAD_PAD]
            k = k_ref[0:kr, hh * MLA_HEAD_PAD:(hh + 1) * MLA_HEAD_PAD]
            s = lax.dot_general(k, q, (((1,), (1,)), ((), ())), preferred_element_type=F32)
            if masked:
                key = lax.broadcasted_iota(jnp.int32, (kr, half), 0)
                qry = lax.broadcasted_iota(jnp.int32, (kr, half), 1) + n0
                s = jnp.where(key <= qry, s, -1e30)
            s_scr[hh, c, 0:kr] = s
            m_half[(hh, c)] = jnp.max(s, axis=0, keepdims=True)

        def update(hh, c):
            cols = slice(c * half, (c + 1) * half)
            vt = jnp.concatenate([vt_ref[hh * MLA_V:(hh + 1) * MLA_V, :], ones], axis=0)
            m_prev = m_scr[hh, :, cols]
            m_new = jnp.maximum(m_prev, m_half[(hh, c)])
            alpha = jnp.exp2(m_prev - m_new)
            upd = None
            for k0 in range(0, visible(c), PV_TILE):
                pexp = jnp.exp2(s_scr[hh, c, k0:k0 + PV_TILE, :] - m_new)
                d = jnp.dot(vt[:, k0:k0 + PV_TILE], pexp.astype(BF16), preferred_element_type=F32)
                upd = d if upd is None else upd + d
            acc_scr[hh, :, cols] = alpha * acc_scr[hh, :, cols] + upd
            m_scr[hh, :, cols] = m_new

        units = [(hh, c) for hh in range(HEADS_PER_STEP) for c in range(2)]
        for u in units[:LOOKAHEAD]:
            scores(*u)
        for t, u in enumerate(units):
            if t + LOOKAHEAD < len(units):
                scores(*units[t + LOOKAHEAD])
            update(*u)

    @pl.when(j < i)
    def _():
        step(False)

    @pl.when(j == i)
    def _():
        step(True)
        o_t = jnp.concatenate([acc_scr[hh, :MLA_V] / acc_scr[hh, MLA_V:MLA_V + 1]
                               for hh in range(HEADS_PER_STEP)], axis=0)
        o_ref[...] = o_t.T.astype(BF16)


def _flash(q, k, vt, batch, seq):
    tq = _row_tile(seq, 512)
    nq = seq // tq
    pairs = [(i, j) for i in range(nq) for j in range(i + 1)]
    qi = jnp.asarray(np.array([p[0] for p in pairs], np.int32))
    kj = jnp.asarray(np.array([p[1] for p in pairs], np.int32))
    qk_w = HEADS_PER_STEP * MLA_HEAD_PAD
    v_w = HEADS_PER_STEP * MLA_V
    grid_spec = pltpu.PrefetchScalarGridSpec(
        num_scalar_prefetch=2,
        grid=(batch, MLA_HEADS // HEADS_PER_STEP, len(pairs)),
        in_specs=[pl.BlockSpec((tq, qk_w), lambda b, h, p, qi, kj: (b * nq + qi[p], h)),
                  pl.BlockSpec((tq, qk_w), lambda b, h, p, qi, kj: (b * nq + kj[p], h)),
                  pl.BlockSpec((None, v_w, tq), lambda b, h, p, qi, kj: (b, h, kj[p]))],
        out_specs=pl.BlockSpec((tq, v_w), lambda b, h, p, qi, kj: (b * nq + qi[p], h)),
        scratch_shapes=[pltpu.VMEM((HEADS_PER_STEP, 1, tq), F32),
                        pltpu.VMEM((HEADS_PER_STEP, V_AUG, tq), F32),
                        pltpu.VMEM((HEADS_PER_STEP, 2, tq, tq // 2), F32)])
    return pl.pallas_call(
        _flash_kernel,
        grid_spec=grid_spec,
        out_shape=jax.ShapeDtypeStruct((batch * seq, MLA_WIDTH), BF16),
        compiler_params=_params("parallel", "parallel", "arbitrary"),
        name="flash_attn",
    )(qi, kj, q, k, vt)


def _out_proj_kernel(ys_ref, u_ref, halo_ref, ym_ref, x_ref, w_ref, pw_ref, ps_ref, nw_ref, xo_ref, h_ref,
                     *, seq):
    a, b = SSD_WIDTH, SSD_WIDTH + POOL_WIDTH
    tm = x_ref.shape[0]
    y_pool = _pool_tile(u_ref, halo_ref, pw_ref, ps_ref, (pl.program_id(0) * tm) % seq)
    acc = x_ref[...]
    acc = acc + jnp.dot(ys_ref[...], w_ref[0:a, :], preferred_element_type=F32)
    acc = acc + jnp.dot(y_pool, w_ref[a:b, :], preferred_element_type=F32)
    acc = acc + jnp.dot(ym_ref[...], w_ref[b:, :], preferred_element_type=F32)
    xo_ref[...] = acc
    h_ref[...] = (_rms_scale(acc) * nw_ref[...]).astype(BF16)


def _out_proj(y_ssd, u, y_mla, x, w_out, pool_w, pool_scale, norm_w, seq):
    t, d = x.shape
    tm = _row_tile(seq, 512)
    assert max(POOL_WINDOWS) - 1 < HALO and tm % HALO == 0
    hb = tm // HALO
    row = lambda n: pl.BlockSpec((tm, n), lambda i: (i, 0))
    return pl.pallas_call(
        functools.partial(_out_proj_kernel, seq=seq),
        grid=(t // tm,),
        in_specs=[row(SSD_WIDTH), row(POOL_WIDTH),
                  pl.BlockSpec((HALO, POOL_WIDTH), lambda i: (jnp.maximum(i * hb - 1, 0), 0)),
                  row(MLA_WIDTH), row(d), _full(w_out.shape),
                  _full((POOL_GROUPS, POOL_GROUP_DIM, POOL_GROUP_DIM)), _full((1, POOL_WIDTH)), _full((1, d))],
        out_specs=[row(d), row(d)],
        out_shape=[jax.ShapeDtypeStruct((t, d), F32), jax.ShapeDtypeStruct((t, d), BF16)],
        compiler_params=_params("parallel"),
        name="out_proj",
    )(y_ssd, u, u, y_mla, x, w_out.astype(BF16), pool_w.astype(BF16), pool_scale.reshape(1, -1),
      norm_w.reshape(1, d))


def _ffn_kernel(h_ref, x_ref, wup_ref, cw_ref, cb_ref, wdn_ref, fn_ref, o_ref, tail_scr,
                *, tiles_per_seq, dff, final_norm):
    tm = h_ref.shape[0]

    @pl.when(pl.program_id(0) % tiles_per_seq == 0)
    def _():
        tail_scr[...] = jnp.zeros_like(tail_scr)

    def up_conv(c0):
        cols = slice(c0, c0 + dff)
        u = jnp.dot(h_ref[...], wup_ref[:, cols], preferred_element_type=F32)
        ue = jnp.concatenate([tail_scr[:, cols], u], axis=0)
        tail_scr[:, cols] = u[tm - 8:, :]
        w = cw_ref[:, cols]
        y = cb_ref[:, cols] + w[FFN_CONV - 1:FFN_CONV] * u
        for j in range(1, FFN_CONV):
            y = y + w[FFN_CONV - 1 - j:FFN_CONV - j] * pltpu.roll(ue, j, axis=0)[8:]
        return y

    a = (_silu(up_conv(0)) * up_conv(dff)).astype(BF16)
    acc = x_ref[...] + jnp.dot(a, wdn_ref[...], preferred_element_type=F32)
    if final_norm:
        acc = _rms_scale(acc) * fn_ref[...]
    o_ref[...] = acc


def _ffn(h, x, w_up, conv_w, conv_b, w_down, final_w, seq, final_norm):
    t, d = x.shape
    dff = w_down.shape[0]
    tm = _row_tile(seq, 512)
    assert FFN_CONV - 1 < 8
    row = lambda n: pl.BlockSpec((tm, n), lambda i: (i, 0))
    return pl.pallas_call(
        functools.partial(_ffn_kernel, tiles_per_seq=seq // tm, dff=dff, final_norm=final_norm),
        grid=(t // tm,),
        in_specs=[row(d), row(d), _resident((d, 2 * dff)), _full((FFN_CONV, 2 * dff)), _full((1, 2 * dff)),
                  _resident((dff, d)), _full((1, d))],
        out_specs=row(d),
        out_shape=jax.ShapeDtypeStruct((t, d), F32),
        scratch_shapes=[pltpu.VMEM((8, 2 * dff), F32)],
        compiler_params=_params("arbitrary"),
        name="ffn",
    )(h, x, w_up.astype(BF16), conv_w, conv_b.reshape(1, -1), w_down.astype(BF16), final_w.reshape(1, d))


def kernel(x, positions, attn_norm, w_in, ssd_conv_w, ssd_conv_b, ssd_dt_bias, ssd_a_log, ssd_d, ssd_norm,
           pool_w, pool_scale, mla_q_norm, mla_w_uq, mla_kv_norm, mla_w_ukv, w_out, ffn_norm, ffn_w_up,
           ffn_conv_w, ffn_conv_b, ffn_w_down, final_norm):
    batch, seq, d = x.shape
    depth = w_in.shape[0]
    assert seq % SSD_CHUNK == 0
    cos_t, sin_t = _rope_tables(positions)
    xf = x.reshape(batch * seq, d)
    for l in range(depth):
        z, xs, bc, u, cq, ckv, small = _in_proj(xf, attn_norm[l], _pack_w_in(w_in[l]),
                                                ssd_conv_w[l], ssd_conv_b[l], seq)
        y_ssd = _ssd(xs, bc, z, small, ssd_dt_bias[l], ssd_a_log[l], ssd_d[l], ssd_norm[l], batch, seq)
        wq, wk, wvt = _pack_mla_weights(mla_w_uq[l], mla_w_ukv[l])
        q, k, vt = _mla_prep(cq, ckv, small, cos_t, sin_t, mla_q_norm[l], mla_kv_norm[l], wq, wk, wvt,
                             batch, seq)
        y_mla = _flash(q, k, vt, batch, seq)
        xf, h = _out_proj(y_ssd, u, y_mla, xf, w_out[l], pool_w[l], pool_scale[l], ffn_norm[l], seq)
        xf = _ffn(h, xf, ffn_w_up[l], ffn_conv_w[l], ffn_conv_b[l], ffn_w_down[l], final_norm, seq,
                  final_norm=(l == depth - 1))
    return xf.reshape(batch, seq, d)
```

```python
import functools
import math

import jax
import jax.numpy as jnp
import numpy as np
from jax import lax
from jax.experimental import pallas as pl
from jax.experimental.pallas import tpu as pltpu

F32 = jnp.float32
BF16 = jnp.bfloat16

EPS = 1e-6
SSD_HEADS = 16
SSD_HEAD_DIM = 64
SSD_WIDTH = SSD_HEADS * SSD_HEAD_DIM
SSD_GROUPS = 2
SSD_HEADS_PER_GROUP = SSD_HEADS // SSD_GROUPS
SSD_STATE = 128
SSD_CONV = 4
SSD_CHUNK = 128
SSD_CONV_CH = SSD_WIDTH + 2 * SSD_GROUPS * SSD_STATE
POOL_GROUPS = 4
POOL_GROUP_DIM = 128
POOL_WIDTH = POOL_GROUPS * POOL_GROUP_DIM
POOL_WINDOWS = (2, 4, 8, 16)
MLA_HEADS = 8
MLA_Q_RANK = 384
MLA_KV_RANK = 256
MLA_NOPE = 64
MLA_ROPE = 32
MLA_V = 64
MLA_QK = MLA_NOPE + MLA_ROPE
MLA_WIDTH = MLA_HEADS * MLA_V
ROPE_THETA = 10000.0
FFN_CONV = 3

LANES = 128
HALO = 16
MLA_HEAD_PAD = LANES
ROPE_HALF = MLA_ROPE // 2
ROPE_X1 = 32
ROPE_X2 = ROPE_X1 + LANES // 2
VMEM_LIMIT = 56 * 1024 * 1024

_C_Z = 0
_C_XBC = _C_Z + SSD_WIDTH
_C_U = _C_XBC + SSD_CONV_CH
_C_CQ = _C_U + POOL_WIDTH
_C_CKV = _C_CQ + MLA_Q_RANK
_C_SMALL = _C_CKV + MLA_KV_RANK
_C_END = _C_SMALL + LANES


def _silu(x):
    h = 0.5 * x
    return h + h * jnp.tanh(h)


def _rms_scale(x):
    return x * lax.rsqrt(jnp.mean(x * x, axis=-1, keepdims=True) + EPS)


def _row_tile(n, want):
    t = min(n, want)
    assert n % t == 0
    return t


def _params(*sem, flags=None):
    return pltpu.CompilerParams(dimension_semantics=sem, vmem_limit_bytes=VMEM_LIMIT, flags=flags)


def _full(shape):
    zeros = (0,) * len(shape)
    return pl.BlockSpec(shape, lambda *_: zeros)


def _resident(shape):
    zeros = (0,) * len(shape)
    return pl.BlockSpec(shape, lambda *_: zeros, pipeline_mode=pl.Buffered(1))


def _rope_kernel(pos_ref, freq_ref, cos_ref, sin_ref):
    pos = pos_ref[...].astype(F32)
    ang = freq_ref[...] * pos
    c = jnp.cos(ang)
    s = jnp.sin(ang)
    ts = pos.shape[-1]
    lo = jnp.zeros((ROPE_X1, ts), F32)
    mid = jnp.zeros((ROPE_X2 - ROPE_X1 - ROPE_HALF, ts), F32)
    hi = jnp.zeros((LANES - ROPE_X2 - ROPE_HALF, ts), F32)
    cos_ref[...] = jnp.concatenate([lo, c, mid, c, hi], axis=0).T
    sin_ref[...] = jnp.concatenate([lo, -s, mid, s, hi], axis=0).T


def _rope_tables(positions):
    b, s = positions.shape
    ts = _row_tile(s, 4096)
    half = MLA_ROPE // 2
    inv_freq = ROPE_THETA ** (-jnp.arange(0, MLA_ROPE, 2, dtype=F32) / MLA_ROPE)
    out = jax.ShapeDtypeStruct((b * s, LANES), F32)
    ns = s // ts
    return pl.pallas_call(
        _rope_kernel,
        grid=(b, ns),
        in_specs=[pl.BlockSpec((None, 1, ts), lambda i, j: (i, 0, j)), _full((half, 1))],
        out_specs=[pl.BlockSpec((ts, LANES), lambda i, j: (i * ns + j, 0))] * 2,
        out_shape=[out, out],
        compiler_params=_params("parallel", "parallel"),
        name="rope_tables",
    )(positions.reshape(b, 1, s), inv_freq.reshape(half, 1))


def _in_proj_kernel(x_ref, nw_ref, w_ref, cw_ref, cb_ref, z_ref, xs_ref, bc_ref, u_ref, cq_ref, ckv_ref,
                    small_ref, tail_scr, *, tiles_per_seq):
    tm = x_ref.shape[0]

    @pl.when(pl.program_id(0) % tiles_per_seq == 0)
    def _():
        tail_scr[...] = jnp.zeros_like(tail_scr)

    h = (_rms_scale(x_ref[...]) * nw_ref[...]).astype(BF16)
    proj = jnp.dot(h, w_ref[...], preferred_element_type=F32)
    z_ref[...] = proj[:, _C_Z:_C_XBC].astype(BF16)
    xbc = proj[:, _C_XBC:_C_U]
    xe = jnp.concatenate([tail_scr[...], xbc], axis=0)
    tail_scr[...] = xbc[tm - 8:, :]
    cw = cw_ref[...]
    conv = cb_ref[...] + cw[SSD_CONV - 1:SSD_CONV] * xbc
    for j in range(1, SSD_CONV):
        conv = conv + cw[SSD_CONV - 1 - j:SSD_CONV - j] * pltpu.roll(xe, j, axis=0)[8:]
    act = _silu(conv).astype(BF16)
    xs_ref[...] = act[:, :SSD_WIDTH]
    bc_ref[...] = act[:, SSD_WIDTH:]
    u_ref[...] = proj[:, _C_U:_C_CQ].astype(BF16)
    cq_ref[...] = proj[:, _C_CQ:_C_CKV].astype(BF16)
    ckv_ref[...] = proj[:, _C_CKV:_C_SMALL].astype(BF16)
    small_ref[...] = proj[:, _C_SMALL:_C_END]


def _pack_w_in(w_in):
    d = w_in.shape[0]
    o = 0
    parts = {}
    for name, n in (("z", SSD_WIDTH), ("xbc", SSD_CONV_CH), ("dt", SSD_HEADS), ("u", POOL_WIDTH),
                    ("cq", MLA_Q_RANK), ("ckv", MLA_KV_RANK), ("kpe", MLA_ROPE)):
        parts[name] = w_in[:, o:o + n]
        o += n
    small = jnp.concatenate([
        parts["dt"], jnp.zeros((d, ROPE_X1 - SSD_HEADS), F32),
        parts["kpe"][:, :ROPE_HALF], jnp.zeros((d, ROPE_X2 - ROPE_X1 - ROPE_HALF), F32),
        parts["kpe"][:, ROPE_HALF:], jnp.zeros((d, LANES - ROPE_X2 - ROPE_HALF), F32)], axis=1)
    return jnp.concatenate([parts["z"], parts["xbc"], parts["u"], parts["cq"], parts["ckv"], small],
                           axis=1).astype(BF16)


def _in_proj(x, norm_w, w_packed, conv_w, conv_b, seq):
    t, d = x.shape
    tm = _row_tile(seq, 512)
    assert SSD_CONV - 1 < 8
    bc_w = SSD_CONV_CH - SSD_WIDTH
    widths = (SSD_WIDTH, SSD_WIDTH, bc_w, POOL_WIDTH, MLA_Q_RANK, MLA_KV_RANK)
    row = lambda n: pl.BlockSpec((tm, n), lambda i: (i, 0))
    return pl.pallas_call(
        functools.partial(_in_proj_kernel, tiles_per_seq=seq // tm),
        grid=(t // tm,),
        in_specs=[row(d), _full((1, d)), _full((d, _C_END)),
                  _full((SSD_CONV, SSD_CONV_CH)), _full((1, SSD_CONV_CH))],
        out_specs=[row(n) for n in widths] + [row(LANES)],
        out_shape=[jax.ShapeDtypeStruct((t, n), BF16) for n in widths]
        + [jax.ShapeDtypeStruct((t, LANES), F32)],
        scratch_shapes=[pltpu.VMEM((8, SSD_CONV_CH), F32)],
        compiler_params=_params("arbitrary"),
        name="in_proj",
    )(x, norm_w.reshape(1, d), w_packed, conv_w, conv_b.reshape(1, -1))


SSD_CHUNKS_PER_STEP = 2
SPLIT = 3


def _pack_split(v):
    out = None
    r = v
    for t in range(SPLIT):
        part = r.astype(BF16).astype(F32)
        r = r - part
        placed = part if t == 0 else pltpu.roll(part, t * SSD_HEADS, axis=1)
        out = placed if out is None else out + placed
    return out.astype(BF16)


def _unpack_split(r):
    out = r
    for t in range(1, SPLIT):
        out = out + pltpu.roll(r, LANES - t * SSD_HEADS, axis=1)
    return out


def _ssd_kernel(xs_ref, bc_ref, z_ref, small_ref, dtb_ref, alog_ref, dskip_ref, nw_ref, spread_ref, bcast_ref,
                y_ref, state_scr):
    L, N, P, E = SSD_CHUNK, SSD_STATE, SSD_HEAD_DIM, SSD_HEADS_PER_GROUP
    GW = E * P

    @pl.when(pl.program_id(1) == 0)
    def _():
        state_scr[...] = jnp.zeros_like(state_scr)

    lane = lax.broadcasted_iota(jnp.int32, (L, LANES), 1)
    row = lax.broadcasted_iota(jnp.int32, (L, LANES), 0)
    causal = lane <= row
    head_lane = lane < SSD_HEADS
    tri = jnp.where(causal, 1.0, 0.0).astype(BF16)
    a_neg = -jnp.exp(alog_ref[...])

    for sub in range(SSD_CHUNKS_PER_STEP):
        rows = slice(sub * L, (sub + 1) * L)
        xs_b = xs_ref[rows, :]
        xs = xs_b.astype(F32)
        pre = small_ref[rows, :] + dtb_ref[...]
        softplus = jnp.maximum(pre, 0.0) + jnp.log1p(jnp.exp(-jnp.abs(pre)))
        dt = jnp.where(head_lane, softplus, 0.0)
        da = dt * a_neg
        cum = jnp.where(head_lane, _unpack_split(
            jnp.dot(tri, _pack_split(da), preferred_element_type=F32)), 0.0)
        cum_t = cum.T
        cum_last = cum[L - 1:L, :]
        dte = jnp.where(head_lane, jnp.exp(cum_last - cum), 0.0)
        ecum = jnp.where(head_lane, jnp.exp(cum), 0.0)
        spread = jnp.dot(jnp.concatenate([_pack_split(dt), _pack_split(dte), _pack_split(ecum)], axis=0),
                         spread_ref[...], preferred_element_type=F32)
        dt_x, dte_x, ecum_x = spread[0:L], spread[L:2 * L], spread[2 * L:3 * L]
        cum_b = jnp.dot(_pack_split(cum), bcast_ref[...], preferred_element_type=F32)
        xdt = xs * dt_x
        xdt_b = xdt.astype(BF16)
        xdt_end = (xdt * dte_x).astype(BF16)

        y_off, y_diag = [], []
        for g in range(SSD_GROUPS):
            b_g = bc_ref[rows, g * N:(g + 1) * N]
            c_g = bc_ref[rows, SSD_GROUPS * N + g * N:SSD_GROUPS * N + (g + 1) * N]
            cb = lax.dot_general(c_g, b_g, (((1,), (1,)), ((), ())), preferred_element_type=F32)
            state = state_scr[g]
            y_off.append(jnp.dot(c_g, state.astype(BF16), preferred_element_type=F32))
            for e in range(0, E, 2):
                h = g * E + e
                ms = []
                for hh in (h, h + 1):
                    seg = cum_b[:, hh * L:(hh + 1) * L] - cum_t[hh:hh + 1, :]
                    ms.append((cb * jnp.exp(jnp.where(causal, seg, -jnp.inf))).astype(BF16))
                x2 = xdt_b[:, h * P:(h + 2) * P]
                zero = jnp.zeros_like(x2)
                rhs = jnp.concatenate([jnp.where(lane < P, x2, zero), jnp.where(lane < P, zero, x2)], axis=0)
                y_diag.append(jnp.dot(jnp.concatenate(ms, axis=1), rhs, preferred_element_type=F32))
            new = jnp.dot(b_g.astype(F32).T.astype(BF16), xdt_end[:, g * GW:(g + 1) * GW],
                          preferred_element_type=F32)
            state_scr[g] = state * ecum_x[L - 1:L, g * GW:(g + 1) * GW] + new

        y = jnp.concatenate(y_diag, axis=1) + jnp.concatenate(y_off, axis=1) * ecum_x + xs * dskip_ref[...]
        gated = y * _silu(z_ref[rows, :].astype(F32))
        y_ref[rows, :] = (_rms_scale(gated) * nw_ref[...]).astype(BF16)


def _ssd(xs, bc, z, small, dt_bias, a_log, d_skip, norm_w, batch, seq):
    L = SSD_CHUNK
    rows = L * math.gcd(SSD_CHUNKS_PER_STEP, seq // L)
    assert rows == L * SSD_CHUNKS_PER_STEP and SPLIT * SSD_HEADS <= LANES
    ns = seq // rows
    pad = lambda v: jnp.pad(v, (0, LANES - SSD_HEADS)).reshape(1, LANES)
    k = np.arange(LANES)
    valid = (k < SPLIT * SSD_HEADS)[:, None]
    spread = (valid & ((k % SSD_HEADS)[:, None] == (np.arange(SSD_WIDTH) // SSD_HEAD_DIM)[None, :]))
    bcast = (valid & ((k % SSD_HEADS)[:, None] == (np.arange(SSD_HEADS * L) // L)[None, :]))
    row = lambda n: pl.BlockSpec((rows, n), lambda b, c: (b * ns + c, 0))
    return pl.pallas_call(
        _ssd_kernel,
        grid=(batch, ns),
        in_specs=[row(SSD_WIDTH), row(SSD_CONV_CH - SSD_WIDTH), row(SSD_WIDTH), row(LANES),
                  _full((1, LANES)), _full((1, LANES)), _full((1, SSD_WIDTH)), _full((1, SSD_WIDTH)),
                  _full((LANES, SSD_WIDTH)), _full((LANES, SSD_HEADS * L))],
        out_specs=row(SSD_WIDTH),
        out_shape=jax.ShapeDtypeStruct((batch * seq, SSD_WIDTH), BF16),
        scratch_shapes=[pltpu.VMEM((SSD_GROUPS, SSD_STATE, SSD_HEADS_PER_GROUP * SSD_HEAD_DIM), F32)],
        compiler_params=_params("parallel", "arbitrary"),
        name="ssd",
    )(xs, bc, z, small, pad(dt_bias), pad(a_log), jnp.repeat(d_skip, SSD_HEAD_DIM).reshape(1, -1),
      norm_w.reshape(1, -1), jnp.asarray(spread, BF16), jnp.asarray(bcast, BF16))


def _pool_tile(u_ref, halo_ref, pw_ref, ps_ref, start):
    tm = u_ref.shape[0]
    x = u_ref[...].astype(F32)
    halo = jnp.where(start == 0, 0.0, halo_ref[...].astype(F32))
    xe = jnp.concatenate([halo, x], axis=0)
    pos = start + lax.broadcasted_iota(jnp.int32, (tm, 1), 0)
    out = []
    for gi, w in enumerate(POOL_WINDOWS):
        cols = slice(gi * POOL_GROUP_DIM, (gi + 1) * POOL_GROUP_DIM)
        acc = xe[:, cols]
        sh = 1
        while sh < w:
            acc = acc + pltpu.roll(acc, sh, axis=0)
            sh *= 2
        cnt = jnp.minimum(pos + 1, w).astype(F32)
        pooled = acc[HALO:] / cnt - x[:, cols]
        yg = jnp.dot(pooled.astype(BF16), pw_ref[gi], preferred_element_type=F32)
        out.append((yg * ps_ref[:, cols]).astype(BF16))
    return jnp.concatenate(out, axis=1)


def _rotate_half(x):
    return pltpu.roll(x, LANES // 2, axis=1)


def _mla_prep_kernel(cq_ref, ckv_ref, small_ref, cos_ref, sin_ref, qn_ref, kvn_ref,
                     wq_ref, wk_ref, wvt_ref, q_ref, k_ref, vt_ref, *, scale):
    tm = cq_ref.shape[0]
    lane = lax.broadcasted_iota(jnp.int32, (tm, LANES), 1)
    cos = cos_ref[...]
    sin = sin_ref[...]
    rope_lane = ((lane >= ROPE_X1) & (lane < ROPE_X1 + ROPE_HALF)) | ((lane >= ROPE_X2) & (lane < ROPE_X2 + ROPE_HALF))
    cos_q = jnp.where(rope_lane, cos, 1.0)
    qn = (_rms_scale(cq_ref[...].astype(F32)) * qn_ref[...]).astype(BF16)
    kvn = (_rms_scale(ckv_ref[...].astype(F32)) * kvn_ref[...]).astype(BF16)
    q = jnp.dot(qn, wq_ref[...], preferred_element_type=F32)
    k = jnp.dot(kvn, wk_ref[...], preferred_element_type=F32)
    vt_ref[...] = lax.dot_general(wvt_ref[...], kvn, (((1,), (1,)), ((), ())),
                                  preferred_element_type=F32).astype(BF16)
    kpe = small_ref[...]
    kpe = kpe * cos + _rotate_half(kpe) * sin
    for h in range(MLA_HEADS):
        cols = slice(h * MLA_HEAD_PAD, (h + 1) * MLA_HEAD_PAD)
        qh = q[:, cols]
        qh = qh * cos_q + _rotate_half(qh) * sin
        q_ref[h] = (qh * scale).astype(BF16)
        k_ref[h] = (k[:, cols] + kpe).astype(BF16)


def _head_layout(nope, x1, x2):
    r, h, _ = nope.shape
    gap = jnp.zeros((r, h, LANES // 2 - ROPE_X1 - ROPE_HALF), nope.dtype)
    if x1 is None:
        x1 = x2 = jnp.zeros((r, h, ROPE_HALF), nope.dtype)
    half = MLA_NOPE // 2
    return jnp.concatenate([nope[:, :, :half], x1, gap, nope[:, :, half:], x2, gap], axis=2).reshape(r, h * LANES)


def _pack_mla_weights(w_uq, w_ukv):
    assert ROPE_X1 == MLA_NOPE // 2
    rq = w_uq.shape[0]
    wq = w_uq.reshape(rq, MLA_HEADS, MLA_QK)
    wq = _head_layout(wq[:, :, :MLA_NOPE], wq[:, :, MLA_NOPE:MLA_NOPE + ROPE_HALF], wq[:, :, MLA_NOPE + ROPE_HALF:])
    rk = w_ukv.shape[0]
    wkv = w_ukv.reshape(rk, MLA_HEADS, MLA_NOPE + MLA_V)
    wk = _head_layout(wkv[:, :, :MLA_NOPE], None, None)
    wvt = wkv[:, :, MLA_NOPE:].reshape(rk, MLA_WIDTH).T
    return wq.astype(BF16), wk.astype(BF16), wvt.astype(BF16)


def _mla_prep(cq, ckv, small, cos_t, sin_t, q_norm, kv_norm, wq, wk, wvt, batch, seq):
    t = cq.shape[0]
    tm = _row_tile(seq, 512)
    ns = seq // tm
    qk_w = MLA_HEADS * MLA_HEAD_PAD
    row = lambda n: pl.BlockSpec((tm, n), lambda i: (i, 0))
    heads = pl.BlockSpec((MLA_HEADS, tm, MLA_HEAD_PAD), lambda i: (0, i, 0))
    return pl.pallas_call(
        functools.partial(_mla_prep_kernel, scale=math.log2(math.e) / math.sqrt(MLA_QK)),
        grid=(t // tm,),
        in_specs=[row(MLA_Q_RANK), row(MLA_KV_RANK), row(LANES), row(LANES), row(LANES),
                  _full((1, MLA_Q_RANK)), _full((1, MLA_KV_RANK)),
                  _full((MLA_Q_RANK, qk_w)), _full((MLA_KV_RANK, qk_w)), _full((MLA_WIDTH, MLA_KV_RANK))],
        out_specs=[heads, heads,
                   pl.BlockSpec((None, MLA_WIDTH, tm), lambda i: (i // ns, 0, i % ns))],
        out_shape=[jax.ShapeDtypeStruct((MLA_HEADS, t, MLA_HEAD_PAD), BF16),
                   jax.ShapeDtypeStruct((MLA_HEADS, t, MLA_HEAD_PAD), BF16),
                   jax.ShapeDtypeStruct((batch, MLA_WIDTH, seq), BF16)],
        compiler_params=_params("parallel"),
        name="mla_prep",
    )(cq, ckv, small, cos_t, sin_t, q_norm.reshape(1, -1), kv_norm.reshape(1, -1), wq, wk, wvt)


HEADS_PER_STEP = 8


LOOKAHEAD = 6
PV_TILE = 256
V_AUG = MLA_V + 16


def _flash_kernel(qi_ref, kj_ref, q_ref, k_ref, vt_ref, o_ref, m_scr, acc_scr, s_scr):
    p = pl.program_id(2)
    i = qi_ref[p]
    j = kj_ref[p]
    tq, tk = q_ref.shape[1], k_ref.shape[1]

    @pl.when(j == 0)
    def _():
        m_scr[...] = jnp.full_like(m_scr, -1e30)
        acc_scr[...] = jnp.zeros_like(acc_scr)

    def step(masked):
        ones = jnp.ones((V_AUG - MLA_V, tk), BF16)
        half = tq // 2
        m_half = {}

        def visible(c):
            return min(tk, (c + 1) * half) if masked else tk

        def scores(hh, c):
            n0 = c * half
            kr = visible(c)
            q = q_ref[hh, n0:n0 + half, :]
            k = k_ref[hh, 0:kr, :]
            s = lax.dot_general(k, q, (((1,), (1,)), ((), ())), preferred_element_type=F32)
            if masked:
                key = lax.broadcasted_iota(jnp.int32, (kr, half), 0)
                qry = lax.broadcasted_iota(jnp.int32, (kr, half), 1) + n0
                s = jnp.where(key <= qry, s, -1e30)
            s_scr[hh, c, 0:kr] = s
            m_half[(hh, c)] = jnp.max(s, axis=0, keepdims=True)

        def update(hh, c):
            cols = slice(c * half, (c + 1) * half)
            vt = jnp.concatenate([vt_ref[hh * MLA_V:(hh + 1) * MLA_V, :], ones], axis=0)
            m_prev = m_scr[hh, :, cols]
            m_new = jnp.maximum(m_prev, m_half[(hh, c)])
            alpha = jnp.exp2(m_prev - m_new)
            upd = None
            for k0 in range(0, visible(c), PV_TILE):
                pexp = jnp.exp2(s_scr[hh, c, k0:k0 + PV_TILE, :] - m_new)
                d = jnp.dot(vt[:, k0:k0 + PV_TILE], pexp.astype(BF16), preferred_element_type=F32)
                upd = d if upd is None else upd + d
            acc_scr[hh, :, cols] = alpha * acc_scr[hh, :, cols] + upd
            m_scr[hh, :, cols] = m_new

        units = [(hh, c) for hh in range(HEADS_PER_STEP) for c in range(2)]
        for u in units[:LOOKAHEAD]:
            scores(*u)
        for t, u in enumerate(units):
            if t + LOOKAHEAD < len(units):
                scores(*units[t + LOOKAHEAD])
            update(*u)

    @pl.when(j < i)
    def _():
        step(False)

    @pl.when(j == i)
    def _():
        step(True)
        o_t = jnp.concatenate([acc_scr[hh, :MLA_V] / acc_scr[hh, MLA_V:MLA_V + 1]
                               for hh in range(HEADS_PER_STEP)], axis=0)
        o_ref[...] = o_t.T.astype(BF16)


def _flash(q, k, vt, batch, seq):
    tq = _row_tile(seq, 512)
    nq = seq // tq
    pairs = [(i, j) for i in range(nq) for j in range(i + 1)]
    qi = jnp.asarray(np.array([p[0] for p in pairs], np.int32))
    kj = jnp.asarray(np.array([p[1] for p in pairs], np.int32))
    v_w = HEADS_PER_STEP * MLA_V
    grid_spec = pltpu.PrefetchScalarGridSpec(
        num_scalar_prefetch=2,
        grid=(batch, MLA_HEADS // HEADS_PER_STEP, len(pairs)),
        in_specs=[pl.BlockSpec((HEADS_PER_STEP, tq, MLA_HEAD_PAD), lambda b, h, p, qi, kj: (h, b * nq + qi[p], 0)),
                  pl.BlockSpec((HEADS_PER_STEP, tq, MLA_HEAD_PAD), lambda b, h, p, qi, kj: (h, b * nq + kj[p], 0)),
                  pl.BlockSpec((None, v_w, tq), lambda b, h, p, qi, kj: (b, h, kj[p]))],
        out_specs=pl.BlockSpec((tq, v_w), lambda b, h, p, qi, kj: (b * nq + qi[p], h)),
        scratch_shapes=[pltpu.VMEM((HEADS_PER_STEP, 1, tq), F32),
                        pltpu.VMEM((HEADS_PER_STEP, V_AUG, tq), F32),
                        pltpu.VMEM((HEADS_PER_STEP, 2, tq, tq // 2), F32)])
    return pl.pallas_call(
        _flash_kernel,
        grid_spec=grid_spec,
        out_shape=jax.ShapeDtypeStruct((batch * seq, MLA_WIDTH), BF16),
        compiler_params=_params("parallel", "parallel", "arbitrary"),
        name="flash_attn",
    )(qi, kj, q, k, vt)


def _out_proj_kernel(ys_ref, u_ref, halo_ref, ym_ref, x_ref, w_ref, pw_ref, ps_ref, nw_ref, xo_ref, h_ref,
                     *, seq):
    a, b = SSD_WIDTH, SSD_WIDTH + POOL_WIDTH
    tm = x_ref.shape[0]
    y_pool = _pool_tile(u_ref, halo_ref, pw_ref, ps_ref, (pl.program_id(0) * tm) % seq)
    acc = x_ref[...]
    acc = acc + jnp.dot(ys_ref[...], w_ref[0:a, :], preferred_element_type=F32)
    acc = acc + jnp.dot(y_pool, w_ref[a:b, :], preferred_element_type=F32)
    acc = acc + jnp.dot(ym_ref[...], w_ref[b:, :], preferred_element_type=F32)
    xo_ref[...] = acc
    h_ref[...] = (_rms_scale(acc) * nw_ref[...]).astype(BF16)


def _out_proj(y_ssd, u, y_mla, x, w_out, pool_w, pool_scale, norm_w, seq):
    t, d = x.shape
    tm = _row_tile(seq, 512)
    assert max(POOL_WINDOWS) - 1 < HALO and tm % HALO == 0
    hb = tm // HALO
    row = lambda n: pl.BlockSpec((tm, n), lambda i: (i, 0))
    return pl.pallas_call(
        functools.partial(_out_proj_kernel, seq=seq),
        grid=(t // tm,),
        in_specs=[row(SSD_WIDTH), row(POOL_WIDTH),
                  pl.BlockSpec((HALO, POOL_WIDTH), lambda i: (jnp.maximum(i * hb - 1, 0), 0)),
                  row(MLA_WIDTH), row(d), _full(w_out.shape),
                  _full((POOL_GROUPS, POOL_GROUP_DIM, POOL_GROUP_DIM)), _full((1, POOL_WIDTH)), _full((1, d))],
        out_specs=[row(d), row(d)],
        out_shape=[jax.ShapeDtypeStruct((t, d), F32), jax.ShapeDtypeStruct((t, d), BF16)],
        compiler_params=_params("parallel"),
        name="out_proj",
    )(y_ssd, u, u, y_mla, x, w_out.astype(BF16), pool_w.astype(BF16), pool_scale.reshape(1, -1),
      norm_w.reshape(1, d))


def _ffn_kernel(h_ref, x_ref, wg_ref, wv_ref, cw_ref, cb_ref, wdn_ref, fn_ref, o_ref, tail_scr,
                *, tiles_per_seq, dff, final_norm):
    tm = h_ref.shape[0]

    @pl.when(pl.program_id(0) % tiles_per_seq == 0)
    def _():
        tail_scr[...] = jnp.zeros_like(tail_scr)

    def up_conv(w_ref, c0):
        cols = slice(c0, c0 + dff)
        u = jnp.dot(h_ref[...], w_ref[...], preferred_element_type=F32)
        ue = jnp.concatenate([tail_scr[:, cols], u], axis=0)
        tail_scr[:, cols] = u[tm - 8:, :]
        w = cw_ref[:, cols]
        y = cb_ref[:, cols] + w[FFN_CONV - 1:FFN_CONV] * u
        for j in range(1, FFN_CONV):
            y = y + w[FFN_CONV - 1 - j:FFN_CONV - j] * pltpu.roll(ue, j, axis=0)[8:]
        return y

    a = (_silu(up_conv(wg_ref, 0)) * up_conv(wv_ref, dff)).astype(BF16)
    acc = x_ref[...] + jnp.dot(a, wdn_ref[...], preferred_element_type=F32)
    if final_norm:
        acc = _rms_scale(acc) * fn_ref[...]
    o_ref[...] = acc


def _ffn(h, x, w_up, conv_w, conv_b, w_down, final_w, seq, final_norm):
    t, d = x.shape
    dff = w_down.shape[0]
    tm = _row_tile(seq, 512)
    assert FFN_CONV - 1 < 8
    row = lambda n: pl.BlockSpec((tm, n), lambda i: (i, 0))
    return pl.pallas_call(
        functools.partial(_ffn_kernel, tiles_per_seq=seq // tm, dff=dff, final_norm=final_norm),
        grid=(t // tm,),
        in_specs=[row(d), row(d), _resident((d, dff)), _resident((d, dff)), _full((FFN_CONV, 2 * dff)),
                  _full((1, 2 * dff)),
                  _resident((dff, d)), _full((1, d))],
        out_specs=row(d),
        out_shape=jax.ShapeDtypeStruct((t, d), F32),
        scratch_shapes=[pltpu.VMEM((8, 2 * dff), F32)],
        compiler_params=_params("arbitrary"),
        name="ffn",
    )(h, x, w_up[:, :dff].astype(BF16), w_up[:, dff:].astype(BF16), conv_w, conv_b.reshape(1, -1),
      w_down.astype(BF16), final_w.reshape(1, d))


def kernel(x, positions, attn_norm, w_in, ssd_conv_w, ssd_conv_b, ssd_dt_bias, ssd_a_log, ssd_d, ssd_norm,
           pool_w, pool_scale, mla_q_norm, mla_w_uq, mla_kv_norm, mla_w_ukv, w_out, ffn_norm, ffn_w_up,
           ffn_conv_w, ffn_conv_b, ffn_w_down, final_norm):
    batch, seq, d = x.shape
    depth = w_in.shape[0]
    assert seq % SSD_CHUNK == 0
    cos_t, sin_t = _rope_tables(positions)
    xf = x.reshape(batch * seq, d)
    for l in range(depth):
        z, xs, bc, u, cq, ckv, small = _in_proj(xf, attn_norm[l], _pack_w_in(w_in[l]),
                                                ssd_conv_w[l], ssd_conv_b[l], seq)
        y_ssd = _ssd(xs, bc, z, small, ssd_dt_bias[l], ssd_a_log[l], ssd_d[l], ssd_norm[l], batch, seq)
        wq, wk, wvt = _pack_mla_weights(mla_w_uq[l], mla_w_ukv[l])
        q, k, vt = _mla_prep(cq, ckv, small, cos_t, sin_t, mla_q_norm[l], mla_kv_norm[l], wq, wk, wvt,
                             batch, seq)
        y_mla = _flash(q, k, vt, batch, seq)
        xf, h = _out_proj(y_ssd, u, y_mla, xf, w_out[l], pool_w[l], pool_scale[l], ffn_norm[l], seq)
        xf = _ffn(h, xf, ffn_w_up[l], ffn_conv_w[l], ffn_conv_b[l], ffn_w_down[l], final_norm, seq,
                  final_norm=(l == depth - 1))
    return xf.reshape(batch, seq, d)
```

```python
import functools
import math

import jax
import jax.numpy as jnp
import numpy as np
from jax import lax
from jax.experimental import pallas as pl
from jax.experimental.pallas import tpu as pltpu

F32 = jnp.float32
BF16 = jnp.bfloat16

EPS = 1e-6
SSD_HEADS = 16
SSD_HEAD_DIM = 64
SSD_WIDTH = SSD_HEADS * SSD_HEAD_DIM
SSD_GROUPS = 2
SSD_HEADS_PER_GROUP = SSD_HEADS // SSD_GROUPS
SSD_STATE = 128
SSD_CONV = 4
SSD_CHUNK = 128
SSD_CONV_CH = SSD_WIDTH + 2 * SSD_GROUPS * SSD_STATE
POOL_GROUPS = 4
POOL_GROUP_DIM = 128
POOL_WIDTH = POOL_GROUPS * POOL_GROUP_DIM
POOL_WINDOWS = (2, 4, 8, 16)
MLA_HEADS = 8
MLA_Q_RANK = 384
MLA_KV_RANK = 256
MLA_NOPE = 64
MLA_ROPE = 32
MLA_V = 64
MLA_QK = MLA_NOPE + MLA_ROPE
MLA_WIDTH = MLA_HEADS * MLA_V
ROPE_THETA = 10000.0
FFN_CONV = 3

LANES = 128
HALO = 16
MLA_HEAD_PAD = LANES
ROPE_HALF = MLA_ROPE // 2
ROPE_X1 = 32
ROPE_X2 = ROPE_X1 + LANES // 2
VMEM_LIMIT = 56 * 1024 * 1024
ROW_TILE = 512
ATTN_TILE = 512
ROPE_TILE = 4096

_C_Z = 0
_C_XBC = _C_Z + SSD_WIDTH
_C_U = _C_XBC + SSD_CONV_CH
_C_CQ = _C_U + POOL_WIDTH
_C_CKV = _C_CQ + MLA_Q_RANK
_C_SMALL = _C_CKV + MLA_KV_RANK
_C_END = _C_SMALL + LANES


def _silu(x):
    h = 0.5 * x
    return h + h * jnp.tanh(h)


def _rms_scale(x):
    return x * lax.rsqrt(jnp.mean(x * x, axis=-1, keepdims=True) + EPS)


def _row_tile(n, want):
    t = min(n, want)
    assert n % t == 0
    return t


def _params(*sem, flags=None):
    return pltpu.CompilerParams(dimension_semantics=sem, vmem_limit_bytes=VMEM_LIMIT, flags=flags)


def _full(shape):
    zeros = (0,) * len(shape)
    return pl.BlockSpec(shape, lambda *_: zeros)


def _resident(shape):
    zeros = (0,) * len(shape)
    return pl.BlockSpec(shape, lambda *_: zeros, pipeline_mode=pl.Buffered(1))


def _rope_kernel(pos_ref, freq_ref, cos_ref, sin_ref):
    pos = pos_ref[...].astype(F32)
    ang = freq_ref[...] * pos
    c = jnp.cos(ang)
    s = jnp.sin(ang)
    ts = pos.shape[-1]
    lo = jnp.zeros((ROPE_X1, ts), F32)
    mid = jnp.zeros((ROPE_X2 - ROPE_X1 - ROPE_HALF, ts), F32)
    hi = jnp.zeros((LANES - ROPE_X2 - ROPE_HALF, ts), F32)
    cos_ref[...] = jnp.concatenate([lo, c, mid, c, hi], axis=0).T
    sin_ref[...] = jnp.concatenate([lo, -s, mid, s, hi], axis=0).T


def _rope_tables(positions):
    b, s = positions.shape
    ts = _row_tile(s, ROPE_TILE)
    half = MLA_ROPE // 2
    inv_freq = ROPE_THETA ** (-jnp.arange(0, MLA_ROPE, 2, dtype=F32) / MLA_ROPE)
    out = jax.ShapeDtypeStruct((b * s, LANES), F32)
    ns = s // ts
    return pl.pallas_call(
        _rope_kernel,
        grid=(b, ns),
        in_specs=[pl.BlockSpec((None, 1, ts), lambda i, j: (i, 0, j)), _full((half, 1))],
        out_specs=[pl.BlockSpec((ts, LANES), lambda i, j: (i * ns + j, 0))] * 2,
        out_shape=[out, out],
        compiler_params=_params("parallel", "parallel"),
        name="rope_tables",
    )(positions.reshape(b, 1, s), inv_freq.reshape(half, 1))


def _in_proj_kernel(x_ref, nw_ref, w_ref, cw_ref, cb_ref, z_ref, xs_ref, bc_ref, u_ref, cq_ref, ckv_ref,
                    small_ref, tail_scr, *, tiles_per_seq):
    tm = x_ref.shape[0]

    @pl.when(pl.program_id(0) % tiles_per_seq == 0)
    def _():
        tail_scr[...] = jnp.zeros_like(tail_scr)

    h = (_rms_scale(x_ref[...]) * nw_ref[...]).astype(BF16)
    proj = jnp.dot(h, w_ref[...], preferred_element_type=F32)
    z_ref[...] = proj[:, _C_Z:_C_XBC].astype(BF16)
    xbc = proj[:, _C_XBC:_C_U]
    xe = jnp.concatenate([tail_scr[...], xbc], axis=0)
    tail_scr[...] = xbc[tm - 8:, :]
    cw = cw_ref[...]
    conv = cb_ref[...] + cw[SSD_CONV - 1:SSD_CONV] * xbc
    for j in range(1, SSD_CONV):
        conv = conv + cw[SSD_CONV - 1 - j:SSD_CONV - j] * pltpu.roll(xe, j, axis=0)[8:]
    act = _silu(conv).astype(BF16)
    xs_ref[...] = act[:, :SSD_WIDTH]
    bc_ref[...] = act[:, SSD_WIDTH:]
    u_ref[...] = proj[:, _C_U:_C_CQ].astype(BF16)
    cq_ref[...] = proj[:, _C_CQ:_C_CKV].astype(BF16)
    ckv_ref[...] = proj[:, _C_CKV:_C_SMALL].astype(BF16)
    small_ref[...] = proj[:, _C_SMALL:_C_END]


def _pack_w_in(w_in):
    d = w_in.shape[0]
    o = 0
    parts = {}
    for name, n in (("z", SSD_WIDTH), ("xbc", SSD_CONV_CH), ("dt", SSD_HEADS), ("u", POOL_WIDTH),
                    ("cq", MLA_Q_RANK), ("ckv", MLA_KV_RANK), ("kpe", MLA_ROPE)):
        parts[name] = w_in[:, o:o + n]
        o += n
    small = jnp.concatenate([
        parts["dt"], jnp.zeros((d, ROPE_X1 - SSD_HEADS), F32),
        parts["kpe"][:, :ROPE_HALF], jnp.zeros((d, ROPE_X2 - ROPE_X1 - ROPE_HALF), F32),
        parts["kpe"][:, ROPE_HALF:], jnp.zeros((d, LANES - ROPE_X2 - ROPE_HALF), F32)], axis=1)
    return jnp.concatenate([parts["z"], parts["xbc"], parts["u"], parts["cq"], parts["ckv"], small],
                           axis=1).astype(BF16)


def _in_proj(x, norm_w, w_packed, conv_w, conv_b, seq):
    t, d = x.shape
    tm = _row_tile(seq, ROW_TILE)
    assert SSD_CONV - 1 < 8
    bc_w = SSD_CONV_CH - SSD_WIDTH
    widths = (SSD_WIDTH, SSD_WIDTH, bc_w, POOL_WIDTH, MLA_Q_RANK, MLA_KV_RANK)
    row = lambda n: pl.BlockSpec((tm, n), lambda i: (i, 0))
    return pl.pallas_call(
        functools.partial(_in_proj_kernel, tiles_per_seq=seq // tm),
        grid=(t // tm,),
        in_specs=[row(d), _full((1, d)), _full((d, _C_END)),
                  _full((SSD_CONV, SSD_CONV_CH)), _full((1, SSD_CONV_CH))],
        out_specs=[row(n) for n in widths] + [row(LANES)],
        out_shape=[jax.ShapeDtypeStruct((t, n), BF16) for n in widths]
        + [jax.ShapeDtypeStruct((t, LANES), F32)],
        scratch_shapes=[pltpu.VMEM((8, SSD_CONV_CH), F32)],
        compiler_params=_params("arbitrary"),
        name="in_proj",
    )(x, norm_w.reshape(1, d), w_packed, conv_w, conv_b.reshape(1, -1))


SSD_CHUNKS_PER_STEP = 2
SPLIT = 3


def _pack_split(v):
    out = None
    r = v
    for t in range(SPLIT):
        part = r.astype(BF16).astype(F32)
        r = r - part
        placed = part if t == 0 else pltpu.roll(part, t * SSD_HEADS, axis=1)
        out = placed if out is None else out + placed
    return out.astype(BF16)


def _unpack_split(r):
    out = r
    for t in range(1, SPLIT):
        out = out + pltpu.roll(r, LANES - t * SSD_HEADS, axis=1)
    return out


def _ssd_kernel(xs_ref, bc_ref, z_ref, small_ref, dtb_ref, alog_ref, dskip_ref, nw_ref, spread_ref, bcast_ref,
                y_ref, state_scr):
    L, N, P, E = SSD_CHUNK, SSD_STATE, SSD_HEAD_DIM, SSD_HEADS_PER_GROUP
    GW = E * P

    @pl.when(pl.program_id(1) == 0)
    def _():
        state_scr[...] = jnp.zeros_like(state_scr)

    lane = lax.broadcasted_iota(jnp.int32, (L, LANES), 1)
    row = lax.broadcasted_iota(jnp.int32, (L, LANES), 0)
    causal = lane <= row
    head_lane = lane < SSD_HEADS
    tri = jnp.where(causal, 1.0, 0.0).astype(BF16)
    a_neg = -jnp.exp(alog_ref[...])

    for sub in range(SSD_CHUNKS_PER_STEP):
        rows = slice(sub * L, (sub + 1) * L)
        xs_b = xs_ref[rows, :]
        xs = xs_b.astype(F32)
        pre = small_ref[rows, :] + dtb_ref[...]
        softplus = jnp.maximum(pre, 0.0) + jnp.log1p(jnp.exp(-jnp.abs(pre)))
        dt = jnp.where(head_lane, softplus, 0.0)
        da = dt * a_neg
        cum = jnp.where(head_lane, _unpack_split(
            jnp.dot(tri, _pack_split(da), preferred_element_type=F32)), 0.0)
        cum_t = cum.T
        cum_last = cum[L - 1:L, :]
        dte = jnp.where(head_lane, jnp.exp(cum_last - cum), 0.0)
        ecum = jnp.where(head_lane, jnp.exp(cum), 0.0)
        spread = jnp.dot(jnp.concatenate([_pack_split(dt), _pack_split(dte), _pack_split(ecum)], axis=0),
                         spread_ref[...], preferred_element_type=F32)
        dt_x, dte_x, ecum_x = spread[0:L], spread[L:2 * L], spread[2 * L:3 * L]
        cum_b = jnp.dot(_pack_split(cum), bcast_ref[...], preferred_element_type=F32)
        xdt = xs * dt_x
        xdt_b = xdt.astype(BF16)
        xdt_end = (xdt * dte_x).astype(BF16)

        y_off, y_diag = [], []
        for g in range(SSD_GROUPS):
            b_g = bc_ref[rows, g * N:(g + 1) * N]
            c_g = bc_ref[rows, SSD_GROUPS * N + g * N:SSD_GROUPS * N + (g + 1) * N]
            cb = lax.dot_general(c_g, b_g, (((1,), (1,)), ((), ())), preferred_element_type=F32)
            state = state_scr[g]
            y_off.append(jnp.dot(c_g, state.astype(BF16), preferred_element_type=F32))
            for e in range(0, E, 2):
                h = g * E + e
                ms = []
                for hh in (h, h + 1):
                    seg = cum_b[:, hh * L:(hh + 1) * L] - cum_t[hh:hh + 1, :]
                    ms.append((cb * jnp.exp(jnp.where(causal, seg, -jnp.inf))).astype(BF16))
                x2 = xdt_b[:, h * P:(h + 2) * P]
                zero = jnp.zeros_like(x2)
                rhs = jnp.concatenate([jnp.where(lane < P, x2, zero), jnp.where(lane < P, zero, x2)], axis=0)
                y_diag.append(jnp.dot(jnp.concatenate(ms, axis=1), rhs, preferred_element_type=F32))
            new = jnp.dot(b_g.astype(F32).T.astype(BF16), xdt_end[:, g * GW:(g + 1) * GW],
                          preferred_element_type=F32)
            state_scr[g] = state * ecum_x[L - 1:L, g * GW:(g + 1) * GW] + new

        y = jnp.concatenate(y_diag, axis=1) + jnp.concatenate(y_off, axis=1) * ecum_x + xs * dskip_ref[...]
        gated = y * _silu(z_ref[rows, :].astype(F32))
        y_ref[rows, :] = (_rms_scale(gated) * nw_ref[...]).astype(BF16)


def _ssd(xs, bc, z, small, dt_bias, a_log, d_skip, norm_w, batch, seq):
    L = SSD_CHUNK
    rows = L * math.gcd(SSD_CHUNKS_PER_STEP, seq // L)
    assert rows == L * SSD_CHUNKS_PER_STEP and SPLIT * SSD_HEADS <= LANES
    ns = seq // rows
    pad = lambda v: jnp.pad(v, (0, LANES - SSD_HEADS)).reshape(1, LANES)
    k = np.arange(LANES)
    valid = (k < SPLIT * SSD_HEADS)[:, None]
    spread = (valid & ((k % SSD_HEADS)[:, None] == (np.arange(SSD_WIDTH) // SSD_HEAD_DIM)[None, :]))
    bcast = (valid & ((k % SSD_HEADS)[:, None] == (np.arange(SSD_HEADS * L) // L)[None, :]))
    row = lambda n: pl.BlockSpec((rows, n), lambda b, c: (b * ns + c, 0))
    return pl.pallas_call(
        _ssd_kernel,
        grid=(batch, ns),
        in_specs=[row(SSD_WIDTH), row(SSD_CONV_CH - SSD_WIDTH), row(SSD_WIDTH), row(LANES),
                  _full((1, LANES)), _full((1, LANES)), _full((1, SSD_WIDTH)), _full((1, SSD_WIDTH)),
                  _full((LANES, SSD_WIDTH)), _full((LANES, SSD_HEADS * L))],
        out_specs=row(SSD_WIDTH),
        out_shape=jax.ShapeDtypeStruct((batch * seq, SSD_WIDTH), BF16),
        scratch_shapes=[pltpu.VMEM((SSD_GROUPS, SSD_STATE, SSD_HEADS_PER_GROUP * SSD_HEAD_DIM), F32)],
        compiler_params=_params("parallel", "arbitrary"),
        name="ssd",
    )(xs, bc, z, small, pad(dt_bias), pad(a_log), jnp.repeat(d_skip, SSD_HEAD_DIM).reshape(1, -1),
      norm_w.reshape(1, -1), jnp.asarray(spread, BF16), jnp.asarray(bcast, BF16))


def _pool_tile(u_ref, halo_ref, pw_ref, ps_ref, start):
    tm = u_ref.shape[0]
    x = u_ref[...].astype(F32)
    halo = jnp.where(start == 0, 0.0, halo_ref[...].astype(F32))
    xe = jnp.concatenate([halo, x], axis=0)
    pos = start + lax.broadcasted_iota(jnp.int32, (tm, 1), 0)
    out = []
    for gi, w in enumerate(POOL_WINDOWS):
        cols = slice(gi * POOL_GROUP_DIM, (gi + 1) * POOL_GROUP_DIM)
        acc = xe[:, cols]
        sh = 1
        while sh < w:
            acc = acc + pltpu.roll(acc, sh, axis=0)
            sh *= 2
        cnt = jnp.minimum(pos + 1, w).astype(F32)
        pooled = acc[HALO:] / cnt - x[:, cols]
        yg = jnp.dot(pooled.astype(BF16), pw_ref[gi], preferred_element_type=F32)
        out.append((yg * ps_ref[:, cols]).astype(BF16))
    return jnp.concatenate(out, axis=1)


def _rotate_half(x):
    return pltpu.roll(x, LANES // 2, axis=1)


def _mla_prep_kernel(cq_ref, ckv_ref, small_ref, cos_ref, sin_ref, qn_ref, kvn_ref,
                     wq_ref, wk_ref, wvt_ref, q_ref, k_ref, vt_ref, *, scale):
    tm = cq_ref.shape[0]
    lane = lax.broadcasted_iota(jnp.int32, (tm, LANES), 1)
    cos = cos_ref[...]
    sin = sin_ref[...]
    rope_lane = ((lane >= ROPE_X1) & (lane < ROPE_X1 + ROPE_HALF)) | ((lane >= ROPE_X2) & (lane < ROPE_X2 + ROPE_HALF))
    cos_q = jnp.where(rope_lane, cos, 1.0)
    qn = (_rms_scale(cq_ref[...].astype(F32)) * qn_ref[...]).astype(BF16)
    kvn = (_rms_scale(ckv_ref[...].astype(F32)) * kvn_ref[...]).astype(BF16)
    q = jnp.dot(qn, wq_ref[...], preferred_element_type=F32)
    k = jnp.dot(kvn, wk_ref[...], preferred_element_type=F32)
    vt_ref[...] = lax.dot_general(wvt_ref[...], kvn, (((1,), (1,)), ((), ())),
                                  preferred_element_type=F32).astype(BF16)
    kpe = small_ref[...]
    kpe = kpe * cos + _rotate_half(kpe) * sin
    for h in range(MLA_HEADS):
        cols = slice(h * MLA_HEAD_PAD, (h + 1) * MLA_HEAD_PAD)
        qh = q[:, cols]
        qh = qh * cos_q + _rotate_half(qh) * sin
        q_ref[h] = (qh * scale).astype(BF16)
        k_ref[h] = (k[:, cols] + kpe).astype(BF16)


def _head_layout(nope, x1, x2):
    r, h, _ = nope.shape
    gap = jnp.zeros((r, h, LANES // 2 - ROPE_X1 - ROPE_HALF), nope.dtype)
    if x1 is None:
        x1 = x2 = jnp.zeros((r, h, ROPE_HALF), nope.dtype)
    half = MLA_NOPE // 2
    return jnp.concatenate([nope[:, :, :half], x1, gap, nope[:, :, half:], x2, gap], axis=2).reshape(r, h * LANES)


def _pack_mla_weights(w_uq, w_ukv):
    assert ROPE_X1 == MLA_NOPE // 2
    rq = w_uq.shape[0]
    wq = w_uq.reshape(rq, MLA_HEADS, MLA_QK)
    wq = _head_layout(wq[:, :, :MLA_NOPE], wq[:, :, MLA_NOPE:MLA_NOPE + ROPE_HALF], wq[:, :, MLA_NOPE + ROPE_HALF:])
    rk = w_ukv.shape[0]
    wkv = w_ukv.reshape(rk, MLA_HEADS, MLA_NOPE + MLA_V)
    wk = _head_layout(wkv[:, :, :MLA_NOPE], None, None)
    wvt = wkv[:, :, MLA_NOPE:].reshape(rk, MLA_WIDTH).T
    return wq.astype(BF16), wk.astype(BF16), wvt.astype(BF16)


def _mla_prep(cq, ckv, small, cos_t, sin_t, q_norm, kv_norm, wq, wk, wvt, batch, seq):
    t = cq.shape[0]
    tm = _row_tile(seq, ROW_TILE)
    ns = seq // tm
    qk_w = MLA_HEADS * MLA_HEAD_PAD
    row = lambda n: pl.BlockSpec((tm, n), lambda i: (i, 0))
    heads = pl.BlockSpec((MLA_HEADS, tm, MLA_HEAD_PAD), lambda i: (0, i, 0))
    return pl.pallas_call(
        functools.partial(_mla_prep_kernel, scale=math.log2(math.e) / math.sqrt(MLA_QK)),
        grid=(t // tm,),
        in_specs=[row(MLA_Q_RANK), row(MLA_KV_RANK), row(LANES), row(LANES), row(LANES),
                  _full((1, MLA_Q_RANK)), _full((1, MLA_KV_RANK)),
                  _full((MLA_Q_RANK, qk_w)), _full((MLA_KV_RANK, qk_w)), _full((MLA_WIDTH, MLA_KV_RANK))],
        out_specs=[heads, heads,
                   pl.BlockSpec((None, MLA_WIDTH, tm), lambda i: (i // ns, 0, i % ns))],
        out_shape=[jax.ShapeDtypeStruct((MLA_HEADS, t, MLA_HEAD_PAD), BF16),
                   jax.ShapeDtypeStruct((MLA_HEADS, t, MLA_HEAD_PAD), BF16),
                   jax.ShapeDtypeStruct((batch, MLA_WIDTH, seq), BF16)],
        compiler_params=_params("parallel"),
        name="mla_prep",
    )(cq, ckv, small, cos_t, sin_t, q_norm.reshape(1, -1), kv_norm.reshape(1, -1), wq, wk, wvt)


HEADS_PER_STEP = 8


LOOKAHEAD = 6
PV_TILE = 256
V_AUG = MLA_V + 16


def _flash_kernel(qi_ref, kj_ref, q_ref, k_ref, vt_ref, o_ref, m_scr, acc_scr, s_scr):
    p = pl.program_id(2)
    i = qi_ref[p]
    j = kj_ref[p]
    tq, tk = q_ref.shape[1], k_ref.shape[1]

    @pl.when(j == 0)
    def _():
        m_scr[...] = jnp.full_like(m_scr, -1e30)
        acc_scr[...] = jnp.zeros_like(acc_scr)

    def step(masked):
        ones = jnp.ones((V_AUG - MLA_V, tk), BF16)
        half = tq // 2
        m_half = {}

        def visible(c):
            return min(tk, (c + 1) * half) if masked else tk

        def scores(hh, c):
            n0 = c * half
            kr = visible(c)
            q = q_ref[hh, n0:n0 + half, :]
            k = k_ref[hh, 0:kr, :]
            s = lax.dot_general(k, q, (((1,), (1,)), ((), ())), preferred_element_type=F32)
            if masked:
                key = lax.broadcasted_iota(jnp.int32, (kr, half), 0)
                qry = lax.broadcasted_iota(jnp.int32, (kr, half), 1) + n0
                s = jnp.where(key <= qry, s, -1e30)
            s_scr[hh, c, 0:kr] = s
            m_half[(hh, c)] = jnp.max(s, axis=0, keepdims=True)

        def update(hh, c):
            cols = slice(c * half, (c + 1) * half)
            vt = jnp.concatenate([vt_ref[hh * MLA_V:(hh + 1) * MLA_V, :], ones], axis=0)
            m_prev = m_scr[hh, :, cols]
            m_new = jnp.maximum(m_prev, m_half[(hh, c)])
            alpha = jnp.exp2(m_prev - m_new)
            upd = None
            for k0 in range(0, visible(c), PV_TILE):
                pexp = jnp.exp2(s_scr[hh, c, k0:k0 + PV_TILE, :] - m_new)
                d = jnp.dot(vt[:, k0:k0 + PV_TILE], pexp.astype(BF16), preferred_element_type=F32)
                upd = d if upd is None else upd + d
            acc_scr[hh, :, cols] = alpha * acc_scr[hh, :, cols] + upd
            m_scr[hh, :, cols] = m_new

        units = [(hh, c) for hh in range(HEADS_PER_STEP) for c in range(2)]
        for u in units[:LOOKAHEAD]:
            scores(*u)
        for t, u in enumerate(units):
            if t + LOOKAHEAD < len(units):
                scores(*units[t + LOOKAHEAD])
            update(*u)

    @pl.when(j < i)
    def _():
        step(False)

    @pl.when(j == i)
    def _():
        step(True)
        o_t = jnp.concatenate([acc_scr[hh, :MLA_V] / acc_scr[hh, MLA_V:MLA_V + 1]
                               for hh in range(HEADS_PER_STEP)], axis=0)
        o_ref[...] = o_t.T.astype(BF16)


def _flash(q, k, vt, batch, seq):
    tq = _row_tile(seq, ATTN_TILE)
    nq = seq // tq
    pairs = [(i, j) for i in range(nq) for j in range(i + 1)]
    qi = jnp.asarray(np.array([p[0] for p in pairs], np.int32))
    kj = jnp.asarray(np.array([p[1] for p in pairs], np.int32))
    v_w = HEADS_PER_STEP * MLA_V
    grid_spec = pltpu.PrefetchScalarGridSpec(
        num_scalar_prefetch=2,
        grid=(batch, MLA_HEADS // HEADS_PER_STEP, len(pairs)),
        in_specs=[pl.BlockSpec((HEADS_PER_STEP, tq, MLA_HEAD_PAD), lambda b, h, p, qi, kj: (h, b * nq + qi[p], 0)),
                  pl.BlockSpec((HEADS_PER_STEP, tq, MLA_HEAD_PAD), lambda b, h, p, qi, kj: (h, b * nq + kj[p], 0)),
                  pl.BlockSpec((None, v_w, tq), lambda b, h, p, qi, kj: (b, h, kj[p]))],
        out_specs=pl.BlockSpec((tq, v_w), lambda b, h, p, qi, kj: (b * nq + qi[p], h)),
        scratch_shapes=[pltpu.VMEM((HEADS_PER_STEP, 1, tq), F32),
                        pltpu.VMEM((HEADS_PER_STEP, V_AUG, tq), F32),
                        pltpu.VMEM((HEADS_PER_STEP, 2, tq, tq // 2), F32)])
    return pl.pallas_call(
        _flash_kernel,
        grid_spec=grid_spec,
        out_shape=jax.ShapeDtypeStruct((batch * seq, MLA_WIDTH), BF16),
        compiler_params=_params("parallel", "parallel", "arbitrary"),
        name="flash_attn",
    )(qi, kj, q, k, vt)


def _mix_ffn_kernel(ys_ref, u_ref, halo_ref, ym_ref, x_ref, wo_ref, pw_ref, ps_ref, nw_ref,
                    wg_ref, wv_ref, cw_ref, cb_ref, wdn_ref, fn_ref, o_ref, tail_scr,
                    *, tiles_per_seq, seq, dff, final_norm):
    tm = x_ref.shape[0]
    a, b = SSD_WIDTH, SSD_WIDTH + POOL_WIDTH

    @pl.when(pl.program_id(0) % tiles_per_seq == 0)
    def _():
        tail_scr[...] = jnp.zeros_like(tail_scr)

    y_pool = _pool_tile(u_ref, halo_ref, pw_ref, ps_ref, (pl.program_id(0) * tm) % seq)
    xn = x_ref[...]
    xn = xn + jnp.dot(ys_ref[...], wo_ref[0:a, :], preferred_element_type=F32)
    xn = xn + jnp.dot(y_pool, wo_ref[a:b, :], preferred_element_type=F32)
    xn = xn + jnp.dot(ym_ref[...], wo_ref[b:, :], preferred_element_type=F32)
    h = (_rms_scale(xn) * nw_ref[...]).astype(BF16)

    def up_conv(w_ref, c0):
        cols = slice(c0, c0 + dff)
        u = jnp.dot(h, w_ref[...], preferred_element_type=F32)
        ue = jnp.concatenate([tail_scr[:, cols], u], axis=0)
        tail_scr[:, cols] = u[tm - 8:, :]
        w = cw_ref[:, cols]
        y = cb_ref[:, cols] + w[FFN_CONV - 1:FFN_CONV] * u
        for j in range(1, FFN_CONV):
            y = y + w[FFN_CONV - 1 - j:FFN_CONV - j] * pltpu.roll(ue, j, axis=0)[8:]
        return y

    act = (_silu(up_conv(wg_ref, 0)) * up_conv(wv_ref, dff)).astype(BF16)
    out = xn + jnp.dot(act, wdn_ref[...], preferred_element_type=F32)
    if final_norm:
        out = _rms_scale(out) * fn_ref[...]
    o_ref[...] = out


def _mix_ffn(y_ssd, u, y_mla, x, w_out, pool_w, pool_scale, norm_w, w_up, conv_w, conv_b, w_down, final_w,
             seq, final_norm):
    t, d = x.shape
    dff = w_down.shape[0]
    tm = _row_tile(seq, ROW_TILE)
    assert FFN_CONV - 1 < 8 and max(POOL_WINDOWS) - 1 < HALO and tm % HALO == 0
    hb = tm // HALO
    row = lambda n: pl.BlockSpec((tm, n), lambda i: (i, 0))
    return pl.pallas_call(
        functools.partial(_mix_ffn_kernel, tiles_per_seq=seq // tm, seq=seq, dff=dff, final_norm=final_norm),
        grid=(t // tm,),
        in_specs=[row(SSD_WIDTH), row(POOL_WIDTH),
                  pl.BlockSpec((HALO, POOL_WIDTH), lambda i: (jnp.maximum(i * hb - 1, 0), 0)),
                  row(MLA_WIDTH), row(d), _resident(w_out.shape),
                  _full((POOL_GROUPS, POOL_GROUP_DIM, POOL_GROUP_DIM)), _full((1, POOL_WIDTH)), _full((1, d)),
                  _resident((d, dff)), _resident((d, dff)), _full((FFN_CONV, 2 * dff)), _full((1, 2 * dff)),
                  _resident((dff, d)), _full((1, d))],
        out_specs=row(d),
        out_shape=jax.ShapeDtypeStruct((t, d), F32),
        scratch_shapes=[pltpu.VMEM((8, 2 * dff), F32)],
        compiler_params=_params("arbitrary"),
        name="mix_ffn",
    )(y_ssd, u, u, y_mla, x, w_out.astype(BF16), pool_w.astype(BF16), pool_scale.reshape(1, -1),
      norm_w.reshape(1, d), w_up[:, :dff].astype(BF16), w_up[:, dff:].astype(BF16), conv_w,
      conv_b.reshape(1, -1), w_down.astype(BF16), final_w.reshape(1, d))


def kernel(x, positions, attn_norm, w_in, ssd_conv_w, ssd_conv_b, ssd_dt_bias, ssd_a_log, ssd_d, ssd_norm,
           pool_w, pool_scale, mla_q_norm, mla_w_uq, mla_kv_norm, mla_w_ukv, w_out, ffn_norm, ffn_w_up,
           ffn_conv_w, ffn_conv_b, ffn_w_down, final_norm):
    batch, seq, d = x.shape
    depth = w_in.shape[0]
    assert seq % SSD_CHUNK == 0
    cos_t, sin_t = _rope_tables(positions)
    xf = x.reshape(batch * seq, d)
    for l in range(depth):
        z, xs, bc, u, cq, ckv, small = _in_proj(xf, attn_norm[l], _pack_w_in(w_in[l]),
                                                ssd_conv_w[l], ssd_conv_b[l], seq)
        y_ssd = _ssd(xs, bc, z, small, ssd_dt_bias[l], ssd_a_log[l], ssd_d[l], ssd_norm[l], batch, seq)
        wq, wk, wvt = _pack_mla_weights(mla_w_uq[l], mla_w_ukv[l])
        q, k, vt = _mla_prep(cq, ckv, small, cos_t, sin_t, mla_q_norm[l], mla_kv_norm[l], wq, wk, wvt,
                             batch, seq)
        y_mla = _flash(q, k, vt, batch, seq)
        xf = _mix_ffn(y_ssd, u, y_mla, xf, w_out[l], pool_w[l], pool_scale[l], ffn_norm[l], ffn_w_up[l],
                      ffn_conv_w[l], ffn_conv_b[l], ffn_w_down[l], final_norm, seq, final_norm=(l == depth - 1))
    return xf.reshape(batch, seq, d)
```

```python
import functools
import math

import jax
import jax.numpy as jnp
import numpy as np
from jax import lax
from jax.experimental import pallas as pl
from jax.experimental.pallas import tpu as pltpu

F32 = jnp.float32
BF16 = jnp.bfloat16

EPS = 1e-6
SSD_HEADS = 16
SSD_HEAD_DIM = 64
SSD_WIDTH = SSD_HEADS * SSD_HEAD_DIM
SSD_GROUPS = 2
SSD_HEADS_PER_GROUP = SSD_HEADS // SSD_GROUPS
SSD_STATE = 128
SSD_CONV = 4
SSD_CHUNK = 128
SSD_CONV_CH = SSD_WIDTH + 2 * SSD_GROUPS * SSD_STATE
POOL_GROUPS = 4
POOL_GROUP_DIM = 128
POOL_WIDTH = POOL_GROUPS * POOL_GROUP_DIM
POOL_WINDOWS = (2, 4, 8, 16)
MLA_HEADS = 8
MLA_Q_RANK = 384
MLA_KV_RANK = 256
MLA_NOPE = 64
MLA_ROPE = 32
MLA_V = 64
MLA_QK = MLA_NOPE + MLA_ROPE
MLA_WIDTH = MLA_HEADS * MLA_V
ROPE_THETA = 10000.0
FFN_CONV = 3

LANES = 128
HALO = 16
MLA_HEAD_PAD = LANES
ROPE_HALF = MLA_ROPE // 2
ROPE_X1 = 32
ROPE_X2 = ROPE_X1 + LANES // 2
VMEM_LIMIT = 56 * 1024 * 1024
ROW_TILE = 512
PROJ_TILE = 1024
ATTN_TILE = 512
ROPE_TILE = 4096

_C_Z = 0
_C_XBC = _C_Z + SSD_WIDTH
_C_U = _C_XBC + SSD_CONV_CH
_C_CQ = _C_U + POOL_WIDTH
_C_CKV = _C_CQ + MLA_Q_RANK
_C_SMALL = _C_CKV + MLA_KV_RANK
_C_END = _C_SMALL + LANES


def _silu(x):
    h = 0.5 * x
    return h + h * jnp.tanh(h)


def _rms_scale(x):
    return x * lax.rsqrt(jnp.mean(x * x, axis=-1, keepdims=True) + EPS)


def _row_tile(n, want):
    t = min(n, want)
    assert n % t == 0
    return t


def _params(*sem, flags=None):
    return pltpu.CompilerParams(dimension_semantics=sem, vmem_limit_bytes=VMEM_LIMIT, flags=flags)


def _full(shape):
    zeros = (0,) * len(shape)
    return pl.BlockSpec(shape, lambda *_: zeros)


def _resident(shape):
    zeros = (0,) * len(shape)
    return pl.BlockSpec(shape, lambda *_: zeros, pipeline_mode=pl.Buffered(1))


def _rope_kernel(pos_ref, freq_ref, cos_ref, sin_ref):
    pos = pos_ref[...].astype(F32)
    ang = freq_ref[...] * pos
    c = jnp.cos(ang)
    s = jnp.sin(ang)
    ts = pos.shape[-1]
    lo = jnp.zeros((ROPE_X1, ts), F32)
    mid = jnp.zeros((ROPE_X2 - ROPE_X1 - ROPE_HALF, ts), F32)
    hi = jnp.zeros((LANES - ROPE_X2 - ROPE_HALF, ts), F32)
    cos_ref[...] = jnp.concatenate([lo, c, mid, c, hi], axis=0).T
    sin_ref[...] = jnp.concatenate([lo, -s, mid, s, hi], axis=0).T


def _rope_tables(positions):
    b, s = positions.shape
    ts = _row_tile(s, ROPE_TILE)
    half = MLA_ROPE // 2
    inv_freq = ROPE_THETA ** (-jnp.arange(0, MLA_ROPE, 2, dtype=F32) / MLA_ROPE)
    out = jax.ShapeDtypeStruct((b * s, LANES), F32)
    ns = s // ts
    return pl.pallas_call(
        _rope_kernel,
        grid=(b, ns),
        in_specs=[pl.BlockSpec((None, 1, ts), lambda i, j: (i, 0, j)), _full((half, 1))],
        out_specs=[pl.BlockSpec((ts, LANES), lambda i, j: (i * ns + j, 0))] * 2,
        out_shape=[out, out],
        compiler_params=_params("parallel", "parallel"),
        name="rope_tables",
    )(positions.reshape(b, 1, s), inv_freq.reshape(half, 1))


def _in_proj_kernel(x_ref, nw_ref, w_ref, cw_ref, cb_ref, z_ref, xs_ref, bc_ref, u_ref, cq_ref, ckv_ref,
                    small_ref, tail_scr, *, tiles_per_seq):
    tm = x_ref.shape[0]

    @pl.when(pl.program_id(0) % tiles_per_seq == 0)
    def _():
        tail_scr[...] = jnp.zeros_like(tail_scr)

    h = (_rms_scale(x_ref[...]) * nw_ref[...]).astype(BF16)
    proj = jnp.dot(h, w_ref[...], preferred_element_type=F32)
    z_ref[...] = proj[:, _C_Z:_C_XBC].astype(BF16)
    xbc = proj[:, _C_XBC:_C_U]
    xe = jnp.concatenate([tail_scr[...], xbc], axis=0)
    tail_scr[...] = xbc[tm - 8:, :]
    cw = cw_ref[...]
    conv = cb_ref[...] + cw[SSD_CONV - 1:SSD_CONV] * xbc
    for j in range(1, SSD_CONV):
        conv = conv + cw[SSD_CONV - 1 - j:SSD_CONV - j] * pltpu.roll(xe, j, axis=0)[8:]
    act = _silu(conv).astype(BF16)
    xs_ref[...] = act[:, :SSD_WIDTH]
    bc_ref[...] = act[:, SSD_WIDTH:]
    u_ref[...] = proj[:, _C_U:_C_CQ].astype(BF16)
    cq_ref[...] = proj[:, _C_CQ:_C_CKV].astype(BF16)
    ckv_ref[...] = proj[:, _C_CKV:_C_SMALL].astype(BF16)
    small_ref[...] = proj[:, _C_SMALL:_C_END]


def _pack_w_in(w_in):
    d = w_in.shape[0]
    o = 0
    parts = {}
    for name, n in (("z", SSD_WIDTH), ("xbc", SSD_CONV_CH), ("dt", SSD_HEADS), ("u", POOL_WIDTH),
                    ("cq", MLA_Q_RANK), ("ckv", MLA_KV_RANK), ("kpe", MLA_ROPE)):
        parts[name] = w_in[:, o:o + n]
        o += n
    small = jnp.concatenate([
        parts["dt"], jnp.zeros((d, ROPE_X1 - SSD_HEADS), F32),
        parts["kpe"][:, :ROPE_HALF], jnp.zeros((d, ROPE_X2 - ROPE_X1 - ROPE_HALF), F32),
        parts["kpe"][:, ROPE_HALF:], jnp.zeros((d, LANES - ROPE_X2 - ROPE_HALF), F32)], axis=1)
    return jnp.concatenate([parts["z"], parts["xbc"], parts["u"], parts["cq"], parts["ckv"], small],
                           axis=1).astype(BF16)


def _in_proj(x, norm_w, w_packed, conv_w, conv_b, seq):
    t, d = x.shape
    tm = _row_tile(seq, PROJ_TILE)
    assert SSD_CONV - 1 < 8
    bc_w = SSD_CONV_CH - SSD_WIDTH
    widths = (SSD_WIDTH, SSD_WIDTH, bc_w, POOL_WIDTH, MLA_Q_RANK, MLA_KV_RANK)
    row = lambda n: pl.BlockSpec((tm, n), lambda i: (i, 0))
    return pl.pallas_call(
        functools.partial(_in_proj_kernel, tiles_per_seq=seq // tm),
        grid=(t // tm,),
        in_specs=[row(d), _full((1, d)), _resident((d, _C_END)),
                  _full((SSD_CONV, SSD_CONV_CH)), _full((1, SSD_CONV_CH))],
        out_specs=[row(n) for n in widths] + [row(LANES)],
        out_shape=[jax.ShapeDtypeStruct((t, n), BF16) for n in widths]
        + [jax.ShapeDtypeStruct((t, LANES), F32)],
        scratch_shapes=[pltpu.VMEM((8, SSD_CONV_CH), F32)],
        compiler_params=_params("arbitrary"),
        name="in_proj",
    )(x, norm_w.reshape(1, d), w_packed, conv_w, conv_b.reshape(1, -1))


SSD_CHUNKS_PER_STEP = 8
SPLIT = 3


def _pack_split(v):
    out = None
    r = v
    for t in range(SPLIT):
        part = r.astype(BF16).astype(F32)
        r = r - part
        placed = part if t == 0 else pltpu.roll(part, t * SSD_HEADS, axis=1)
        out = placed if out is None else out + placed
    return out.astype(BF16)


def _unpack_split(r):
    out = r
    for t in range(1, SPLIT):
        out = out + pltpu.roll(r, LANES - t * SSD_HEADS, axis=1)
    return out


def _ssd_kernel(xs_ref, bc_ref, z_ref, small_ref, dtb_ref, alog_ref, dskip_ref, nw_ref, spread_ref, bcast_ref,
                y_ref, state_scr):
    L, N, P, E = SSD_CHUNK, SSD_STATE, SSD_HEAD_DIM, SSD_HEADS_PER_GROUP
    GW = E * P

    @pl.when(pl.program_id(1) == 0)
    def _():
        state_scr[...] = jnp.zeros_like(state_scr)

    lane = lax.broadcasted_iota(jnp.int32, (L, LANES), 1)
    row = lax.broadcasted_iota(jnp.int32, (L, LANES), 0)
    causal = lane <= row
    head_lane = lane < SSD_HEADS
    tri = jnp.where(causal, 1.0, 0.0).astype(BF16)
    a_neg = -jnp.exp(alog_ref[...])

    for sub in range(SSD_CHUNKS_PER_STEP):
        rows = slice(sub * L, (sub + 1) * L)
        xs_b = xs_ref[rows, :]
        xs = xs_b.astype(F32)
        pre = small_ref[rows, :] + dtb_ref[...]
        softplus = jnp.maximum(pre, 0.0) + jnp.log1p(jnp.exp(-jnp.abs(pre)))
        dt = jnp.where(head_lane, softplus, 0.0)
        da = dt * a_neg
        cum = jnp.where(head_lane, _unpack_split(
            jnp.dot(tri, _pack_split(da), preferred_element_type=F32)), 0.0)
        cum_t = cum.T
        cum_last = cum[L - 1:L, :]
        dte = jnp.where(head_lane, jnp.exp(cum_last - cum), 0.0)
        ecum = jnp.where(head_lane, jnp.exp(cum), 0.0)
        spread = jnp.dot(jnp.concatenate([_pack_split(dt), _pack_split(dte), _pack_split(ecum)], axis=0),
                         spread_ref[...], preferred_element_type=F32)
        dt_x, dte_x, ecum_x = spread[0:L], spread[L:2 * L], spread[2 * L:3 * L]
        cum_b = jnp.dot(_pack_split(cum), bcast_ref[...], preferred_element_type=F32)
        xdt = xs * dt_x
        xdt_b = xdt.astype(BF16)
        xdt_end = (xdt * dte_x).astype(BF16)

        y_off, y_diag = [], []
        for g in range(SSD_GROUPS):
            b_g = bc_ref[rows, g * N:(g + 1) * N]
            c_g = bc_ref[rows, SSD_GROUPS * N + g * N:SSD_GROUPS * N + (g + 1) * N]
            cb = lax.dot_general(c_g, b_g, (((1,), (1,)), ((), ())), preferred_element_type=F32)
            state = state_scr[g]
            y_off.append(jnp.dot(c_g, state.astype(BF16), preferred_element_type=F32))
            for e in range(0, E, 2):
                h = g * E + e
                ms = []
                for hh in (h, h + 1):
                    seg = cum_b[:, hh * L:(hh + 1) * L] - cum_t[hh:hh + 1, :]
                    ms.append((cb * jnp.exp(jnp.where(causal, seg, -jnp.inf))).astype(BF16))
                x2 = xdt_b[:, h * P:(h + 2) * P]
                zero = jnp.zeros_like(x2)
                rhs = jnp.concatenate([jnp.where(lane < P, x2, zero), jnp.where(lane < P, zero, x2)], axis=0)
                y_diag.append(jnp.dot(jnp.concatenate(ms, axis=1), rhs, preferred_element_type=F32))
            new = jnp.dot(b_g.astype(F32).T.astype(BF16), xdt_end[:, g * GW:(g + 1) * GW],
                          preferred_element_type=F32)
            state_scr[g] = state * ecum_x[L - 1:L, g * GW:(g + 1) * GW] + new

        y = jnp.concatenate(y_diag, axis=1) + jnp.concatenate(y_off, axis=1) * ecum_x + xs * dskip_ref[...]
        gated = y * _silu(z_ref[rows, :].astype(F32))
        y_ref[rows, :] = (_rms_scale(gated) * nw_ref[...]).astype(BF16)


def _ssd(xs, bc, z, small, dt_bias, a_log, d_skip, norm_w, batch, seq):
    L = SSD_CHUNK
    rows = L * math.gcd(SSD_CHUNKS_PER_STEP, seq // L)
    assert rows == L * SSD_CHUNKS_PER_STEP and SPLIT * SSD_HEADS <= LANES
    ns = seq // rows
    pad = lambda v: jnp.pad(v, (0, LANES - SSD_HEADS)).reshape(1, LANES)
    k = np.arange(LANES)
    valid = (k < SPLIT * SSD_HEADS)[:, None]
    spread = (valid & ((k % SSD_HEADS)[:, None] == (np.arange(SSD_WIDTH) // SSD_HEAD_DIM)[None, :]))
    bcast = (valid & ((k % SSD_HEADS)[:, None] == (np.arange(SSD_HEADS * L) // L)[None, :]))
    row = lambda n: pl.BlockSpec((rows, n), lambda b, c: (b * ns + c, 0))
    return pl.pallas_call(
        _ssd_kernel,
        grid=(batch, ns),
        in_specs=[row(SSD_WIDTH), row(SSD_CONV_CH - SSD_WIDTH), row(SSD_WIDTH), row(LANES),
                  _full((1, LANES)), _full((1, LANES)), _full((1, SSD_WIDTH)), _full((1, SSD_WIDTH)),
                  _full((LANES, SSD_WIDTH)), _full((LANES, SSD_HEADS * L))],
        out_specs=row(SSD_WIDTH),
        out_shape=jax.ShapeDtypeStruct((batch * seq, SSD_WIDTH), BF16),
        scratch_shapes=[pltpu.VMEM((SSD_GROUPS, SSD_STATE, SSD_HEADS_PER_GROUP * SSD_HEAD_DIM), F32)],
        compiler_params=_params("parallel", "arbitrary"),
        name="ssd",
    )(xs, bc, z, small, pad(dt_bias), pad(a_log), jnp.repeat(d_skip, SSD_HEAD_DIM).reshape(1, -1),
      norm_w.reshape(1, -1), jnp.asarray(spread, BF16), jnp.asarray(bcast, BF16))


def _pool_tile(u_ref, halo_ref, pw_ref, ps_ref, start):
    tm = u_ref.shape[0]
    x = u_ref[...].astype(F32)
    halo = jnp.where(start == 0, 0.0, halo_ref[...].astype(F32))
    xe = jnp.concatenate([halo, x], axis=0)
    pos = start + lax.broadcasted_iota(jnp.int32, (tm, 1), 0)
    out = []
    for gi, w in enumerate(POOL_WINDOWS):
        cols = slice(gi * POOL_GROUP_DIM, (gi + 1) * POOL_GROUP_DIM)
        acc = xe[:, cols]
        sh = 1
        while sh < w:
            acc = acc + pltpu.roll(acc, sh, axis=0)
            sh *= 2
        cnt = jnp.minimum(pos + 1, w).astype(F32)
        pooled = acc[HALO:] / cnt - x[:, cols]
        yg = jnp.dot(pooled.astype(BF16), pw_ref[gi], preferred_element_type=F32)
        out.append((yg * ps_ref[:, cols]).astype(BF16))
    return jnp.concatenate(out, axis=1)


def _rotate_half(x):
    return pltpu.roll(x, LANES // 2, axis=1)


def _mla_prep_kernel(cq_ref, ckv_ref, small_ref, cos_ref, sin_ref, qn_ref, kvn_ref,
                     wq_ref, wk_ref, wvt_ref, q_ref, k_ref, vt_ref, *, scale):
    tm = cq_ref.shape[0]
    lane = lax.broadcasted_iota(jnp.int32, (tm, LANES), 1)
    cos = cos_ref[...]
    sin = sin_ref[...]
    rope_lane = ((lane >= ROPE_X1) & (lane < ROPE_X1 + ROPE_HALF)) | ((lane >= ROPE_X2) & (lane < ROPE_X2 + ROPE_HALF))
    cos_q = jnp.where(rope_lane, cos, 1.0)
    qn = (_rms_scale(cq_ref[...].astype(F32)) * qn_ref[...]).astype(BF16)
    kvn = (_rms_scale(ckv_ref[...].astype(F32)) * kvn_ref[...]).astype(BF16)
    q = jnp.dot(qn, wq_ref[...], preferred_element_type=F32)
    k = jnp.dot(kvn, wk_ref[...], preferred_element_type=F32)
    vt_ref[...] = lax.dot_general(wvt_ref[...], kvn, (((1,), (1,)), ((), ())),
                                  preferred_element_type=F32).astype(BF16)
    kpe = small_ref[...]
    kpe = kpe * cos + _rotate_half(kpe) * sin
    for h in range(MLA_HEADS):
        cols = slice(h * MLA_HEAD_PAD, (h + 1) * MLA_HEAD_PAD)
        qh = q[:, cols]
        qh = qh * cos_q + _rotate_half(qh) * sin
        q_ref[h] = (qh * scale).astype(BF16)
        k_ref[h] = (k[:, cols] + kpe).astype(BF16)


def _head_layout(nope, x1, x2):
    r, h, _ = nope.shape
    gap = jnp.zeros((r, h, LANES // 2 - ROPE_X1 - ROPE_HALF), nope.dtype)
    if x1 is None:
        x1 = x2 = jnp.zeros((r, h, ROPE_HALF), nope.dtype)
    half = MLA_NOPE // 2
    return jnp.concatenate([nope[:, :, :half], x1, gap, nope[:, :, half:], x2, gap], axis=2).reshape(r, h * LANES)


def _pack_mla_weights(w_uq, w_ukv):
    assert ROPE_X1 == MLA_NOPE // 2
    rq = w_uq.shape[0]
    wq = w_uq.reshape(rq, MLA_HEADS, MLA_QK)
    wq = _head_layout(wq[:, :, :MLA_NOPE], wq[:, :, MLA_NOPE:MLA_NOPE + ROPE_HALF], wq[:, :, MLA_NOPE + ROPE_HALF:])
    rk = w_ukv.shape[0]
    wkv = w_ukv.reshape(rk, MLA_HEADS, MLA_NOPE + MLA_V)
    wk = _head_layout(wkv[:, :, :MLA_NOPE], None, None)
    wvt = wkv[:, :, MLA_NOPE:].reshape(rk, MLA_WIDTH).T
    return wq.astype(BF16), wk.astype(BF16), wvt.astype(BF16)


def _mla_prep(cq, ckv, small, cos_t, sin_t, q_norm, kv_norm, wq, wk, wvt, batch, seq):
    t = cq.shape[0]
    tm = _row_tile(seq, PROJ_TILE)
    ns = seq // tm
    qk_w = MLA_HEADS * MLA_HEAD_PAD
    row = lambda n: pl.BlockSpec((tm, n), lambda i: (i, 0))
    heads = pl.BlockSpec((MLA_HEADS, tm, MLA_HEAD_PAD), lambda i: (0, i, 0))
    return pl.pallas_call(
        functools.partial(_mla_prep_kernel, scale=math.log2(math.e) / math.sqrt(MLA_QK)),
        grid=(t // tm,),
        in_specs=[row(MLA_Q_RANK), row(MLA_KV_RANK), row(LANES), row(LANES), row(LANES),
                  _full((1, MLA_Q_RANK)), _full((1, MLA_KV_RANK)),
                  _full((MLA_Q_RANK, qk_w)), _full((MLA_KV_RANK, qk_w)), _full((MLA_WIDTH, MLA_KV_RANK))],
        out_specs=[heads, heads,
                   pl.BlockSpec((None, MLA_WIDTH, tm), lambda i: (i // ns, 0, i % ns))],
        out_shape=[jax.ShapeDtypeStruct((MLA_HEADS, t, MLA_HEAD_PAD), BF16),
                   jax.ShapeDtypeStruct((MLA_HEADS, t, MLA_HEAD_PAD), BF16),
                   jax.ShapeDtypeStruct((batch, MLA_WIDTH, seq), BF16)],
        compiler_params=_params("parallel"),
        name="mla_prep",
    )(cq, ckv, small, cos_t, sin_t, q_norm.reshape(1, -1), kv_norm.reshape(1, -1), wq, wk, wvt)


HEADS_PER_STEP = 8


LOOKAHEAD = 6
PV_TILE = 256
V_AUG = MLA_V + 16


def _flash_kernel(qi_ref, kj_ref, q_ref, k_ref, vt_ref, o_ref, m_scr, acc_scr, s_scr):
    p = pl.program_id(2)
    i = qi_ref[p]
    j = kj_ref[p]
    tq, tk = q_ref.shape[1], k_ref.shape[1]

    @pl.when(j == 0)
    def _():
        m_scr[...] = jnp.full_like(m_scr, -1e30)
        acc_scr[...] = jnp.zeros_like(acc_scr)

    def step(masked):
        ones = jnp.ones((V_AUG - MLA_V, tk), BF16)
        half = tq // 2
        m_half = {}

        def visible(c):
            return min(tk, (c + 1) * half) if masked else tk

        def scores(hh, c):
            n0 = c * half
            kr = visible(c)
            q = q_ref[hh, n0:n0 + half, :]
            k = k_ref[hh, 0:kr, :]
            s = lax.dot_general(k, q, (((1,), (1,)), ((), ())), preferred_element_type=F32)
            if masked:
                key = lax.broadcasted_iota(jnp.int32, (kr, half), 0)
                qry = lax.broadcasted_iota(jnp.int32, (kr, half), 1) + n0
                s = jnp.where(key <= qry, s, -1e30)
            s_scr[hh, c, 0:kr] = s
            m_half[(hh, c)] = jnp.max(s, axis=0, keepdims=True)

        def update(hh, c):
            cols = slice(c * half, (c + 1) * half)
            vt = jnp.concatenate([vt_ref[hh * MLA_V:(hh + 1) * MLA_V, :], ones], axis=0)
            m_prev = m_scr[hh, :, cols]
            m_new = jnp.maximum(m_prev, m_half[(hh, c)])
            alpha = jnp.exp2(m_prev - m_new)
            upd = None
            for k0 in range(0, visible(c), PV_TILE):
                pexp = jnp.exp2(s_scr[hh, c, k0:k0 + PV_TILE, :] - m_new)
                d = jnp.dot(vt[:, k0:k0 + PV_TILE], pexp.astype(BF16), preferred_element_type=F32)
                upd = d if upd is None else upd + d
            acc_scr[hh, :, cols] = alpha * acc_scr[hh, :, cols] + upd
            m_scr[hh, :, cols] = m_new

        units = [(hh, c) for hh in range(HEADS_PER_STEP) for c in range(2)]
        for u in units[:LOOKAHEAD]:
            scores(*u)
        for t, u in enumerate(units):
            if t + LOOKAHEAD < len(units):
                scores(*units[t + LOOKAHEAD])
            update(*u)

    @pl.when(j < i)
    def _():
        step(False)

    @pl.when(j == i)
    def _():
        step(True)
        o_t = jnp.concatenate([acc_scr[hh, :MLA_V] / acc_scr[hh, MLA_V:MLA_V + 1]
                               for hh in range(HEADS_PER_STEP)], axis=0)
        o_ref[...] = o_t.T.astype(BF16)


def _flash(q, k, vt, batch, seq):
    tq = _row_tile(seq, ATTN_TILE)
    nq = seq // tq
    pairs = [(i, j) for i in range(nq) for j in range(i + 1)]
    qi = jnp.asarray(np.array([p[0] for p in pairs], np.int32))
    kj = jnp.asarray(np.array([p[1] for p in pairs], np.int32))
    v_w = HEADS_PER_STEP * MLA_V
    grid_spec = pltpu.PrefetchScalarGridSpec(
        num_scalar_prefetch=2,
        grid=(batch, MLA_HEADS // HEADS_PER_STEP, len(pairs)),
        in_specs=[pl.BlockSpec((HEADS_PER_STEP, tq, MLA_HEAD_PAD), lambda b, h, p, qi, kj: (h, b * nq + qi[p], 0)),
                  pl.BlockSpec((HEADS_PER_STEP, tq, MLA_HEAD_PAD), lambda b, h, p, qi, kj: (h, b * nq + kj[p], 0)),
                  pl.BlockSpec((None, v_w, tq), lambda b, h, p, qi, kj: (b, h, kj[p]))],
        out_specs=pl.BlockSpec((tq, v_w), lambda b, h, p, qi, kj: (b * nq + qi[p], h)),
        scratch_shapes=[pltpu.VMEM((HEADS_PER_STEP, 1, tq), F32),
                        pltpu.VMEM((HEADS_PER_STEP, V_AUG, tq), F32),
                        pltpu.VMEM((HEADS_PER_STEP, 2, tq, tq // 2), F32)])
    return pl.pallas_call(
        _flash_kernel,
        grid_spec=grid_spec,
        out_shape=jax.ShapeDtypeStruct((batch * seq, MLA_WIDTH), BF16),
        compiler_params=_params("parallel", "parallel", "arbitrary"),
        name="flash_attn",
    )(qi, kj, q, k, vt)


def _mix_ffn_kernel(ys_ref, u_ref, halo_ref, ym_ref, x_ref, wo_ref, pw_ref, ps_ref, nw_ref,
                    wg_ref, wv_ref, cw_ref, cb_ref, wdn_ref, fn_ref, o_ref, tail_scr,
                    *, tiles_per_seq, seq, dff, final_norm):
    tm = x_ref.shape[0]
    a, b = SSD_WIDTH, SSD_WIDTH + POOL_WIDTH

    @pl.when(pl.program_id(0) % tiles_per_seq == 0)
    def _():
        tail_scr[...] = jnp.zeros_like(tail_scr)

    y_pool = _pool_tile(u_ref, halo_ref, pw_ref, ps_ref, (pl.program_id(0) * tm) % seq)
    xn = x_ref[...]
    xn = xn + jnp.dot(ys_ref[...], wo_ref[0:a, :], preferred_element_type=F32)
    xn = xn + jnp.dot(y_pool, wo_ref[a:b, :], preferred_element_type=F32)
    xn = xn + jnp.dot(ym_ref[...], wo_ref[b:, :], preferred_element_type=F32)
    h = (_rms_scale(xn) * nw_ref[...]).astype(BF16)

    def up_conv(w_ref, c0):
        cols = slice(c0, c0 + dff)
        u = jnp.dot(h, w_ref[...], preferred_element_type=F32)
        ue = jnp.concatenate([tail_scr[:, cols], u], axis=0)
        tail_scr[:, cols] = u[tm - 8:, :]
        w = cw_ref[:, cols]
        y = cb_ref[:, cols] + w[FFN_CONV - 1:FFN_CONV] * u
        for j in range(1, FFN_CONV):
            y = y + w[FFN_CONV - 1 - j:FFN_CONV - j] * pltpu.roll(ue, j, axis=0)[8:]
        return y

    act = (_silu(up_conv(wg_ref, 0)) * up_conv(wv_ref, dff)).astype(BF16)
    out = xn + jnp.dot(act, wdn_ref[...], preferred_element_type=F32)
    if final_norm:
        out = _rms_scale(out) * fn_ref[...]
    o_ref[...] = out


def _mix_ffn(y_ssd, u, y_mla, x, w_out, pool_w, pool_scale, norm_w, w_up, conv_w, conv_b, w_down, final_w,
             seq, final_norm):
    t, d = x.shape
    dff = w_down.shape[0]
    tm = _row_tile(seq, ROW_TILE)
    assert FFN_CONV - 1 < 8 and max(POOL_WINDOWS) - 1 < HALO and tm % HALO == 0
    hb = tm // HALO
    row = lambda n: pl.BlockSpec((tm, n), lambda i: (i, 0))
    return pl.pallas_call(
        functools.partial(_mix_ffn_kernel, tiles_per_seq=seq // tm, seq=seq, dff=dff, final_norm=final_norm),
        grid=(t // tm,),
        in_specs=[row(SSD_WIDTH), row(POOL_WIDTH),
                  pl.BlockSpec((HALO, POOL_WIDTH), lambda i: (jnp.maximum(i * hb - 1, 0), 0)),
                  row(MLA_WIDTH), row(d), _resident(w_out.shape),
                  _full((POOL_GROUPS, POOL_GROUP_DIM, POOL_GROUP_DIM)), _full((1, POOL_WIDTH)), _full((1, d)),
                  _resident((d, dff)), _resident((d, dff)), _full((FFN_CONV, 2 * dff)), _full((1, 2 * dff)),
                  _resident((dff, d)), _full((1, d))],
        out_specs=row(d),
        out_shape=jax.ShapeDtypeStruct((t, d), F32),
        scratch_shapes=[pltpu.VMEM((8, 2 * dff), F32)],
        compiler_params=_params("arbitrary"),
        name="mix_ffn",
    )(y_ssd, u, u, y_mla, x, w_out.astype(BF16), pool_w.astype(BF16), pool_scale.reshape(1, -1),
      norm_w.reshape(1, d), w_up[:, :dff].astype(BF16), w_up[:, dff:].astype(BF16), conv_w,
      conv_b.reshape(1, -1), w_down.astype(BF16), final_w.reshape(1, d))


def kernel(x, positions, attn_norm, w_in, ssd_conv_w, ssd_conv_b, ssd_dt_bias, ssd_a_log, ssd_d, ssd_norm,
           pool_w, pool_scale, mla_q_norm, mla_w_uq, mla_kv_norm, mla_w_ukv, w_out, ffn_norm, ffn_w_up,
           ffn_conv_w, ffn_conv_b, ffn_w_down, final_norm):
    batch, seq, d = x.shape
    depth = w_in.shape[0]
    assert seq % SSD_CHUNK == 0
    cos_t, sin_t = _rope_tables(positions)
    xf = x.reshape(batch * seq, d)
    for l in range(depth):
        z, xs, bc, u, cq, ckv, small = _in_proj(xf, attn_norm[l], _pack_w_in(w_in[l]),
                                                ssd_conv_w[l], ssd_conv_b[l], seq)
        y_ssd = _ssd(xs, bc, z, small, ssd_dt_bias[l], ssd_a_log[l], ssd_d[l], ssd_norm[l], batch, seq)
        wq, wk, wvt = _pack_mla_weights(mla_w_uq[l], mla_w_ukv[l])
        q, k, vt = _mla_prep(cq, ckv, small, cos_t, sin_t, mla_q_norm[l], mla_kv_norm[l], wq, wk, wvt,
                             batch, seq)
        y_mla = _flash(q, k, vt, batch, seq)
        xf = _mix_ffn(y_ssd, u, y_mla, xf, w_out[l], pool_w[l], pool_scale[l], ffn_norm[l], ffn_w_up[l],
                      ffn_conv_w[l], ffn_conv_b[l], ffn_w_down[l], final_norm, seq, final_norm=(l == depth - 1))
    return xf.reshape(batch, seq, d)
```

```python
import functools
import math

import jax
import jax.numpy as jnp
import numpy as np
from jax import lax
from jax.experimental import pallas as pl
from jax.experimental.pallas import tpu as pltpu

F32 = jnp.float32
BF16 = jnp.bfloat16

EPS = 1e-6
SSD_HEADS = 16
SSD_HEAD_DIM = 64
SSD_WIDTH = SSD_HEADS * SSD_HEAD_DIM
SSD_GROUPS = 2
SSD_HEADS_PER_GROUP = SSD_HEADS // SSD_GROUPS
SSD_STATE = 128
SSD_CONV = 4
SSD_CHUNK = 128
SSD_CONV_CH = SSD_WIDTH + 2 * SSD_GROUPS * SSD_STATE
POOL_GROUPS = 4
POOL_GROUP_DIM = 128
POOL_WIDTH = POOL_GROUPS * POOL_GROUP_DIM
POOL_WINDOWS = (2, 4, 8, 16)
MLA_HEADS = 8
MLA_Q_RANK = 384
MLA_KV_RANK = 256
MLA_NOPE = 64
MLA_ROPE = 32
MLA_V = 64
MLA_QK = MLA_NOPE + MLA_ROPE
MLA_WIDTH = MLA_HEADS * MLA_V
ROPE_THETA = 10000.0
FFN_CONV = 3

LANES = 128
HALO = 16
TAIL = 8
MLA_HEAD_PAD = LANES
ROPE_HALF = MLA_ROPE // 2
ROPE_X1 = 32
ROPE_X2 = ROPE_X1 + LANES // 2
VMEM_LIMIT = 56 * 1024 * 1024
ROW_TILE = 512
PROJ_TILE = 1024
ATTN_TILE = 512
ROPE_TILE = 4096

_C_Z = 0
_C_XBC = _C_Z + SSD_WIDTH
_C_U = _C_XBC + SSD_CONV_CH
_C_CQ = _C_U + POOL_WIDTH
_C_CKV = _C_CQ + MLA_Q_RANK
_C_SMALL = _C_CKV + MLA_KV_RANK
_C_END = _C_SMALL + LANES


def _silu(x):
    h = 0.5 * x
    return h + h * jnp.tanh(h)


def _rms_scale(x):
    return x * lax.rsqrt(jnp.mean(x * x, axis=-1, keepdims=True) + EPS)


def _row_tile(n, want):
    t = min(n, want)
    assert n % t == 0
    return t


def _params(*sem):
    return pltpu.CompilerParams(dimension_semantics=sem, vmem_limit_bytes=VMEM_LIMIT)


def _full(shape):
    zeros = (0,) * len(shape)
    return pl.BlockSpec(shape, lambda *_: zeros)


def _resident(shape):
    zeros = (0,) * len(shape)
    return pl.BlockSpec(shape, lambda *_: zeros, pipeline_mode=pl.Buffered(1))


def _rope_kernel(pos_ref, freq_ref, cos_ref, sin_ref):
    pos = pos_ref[...].astype(F32)
    ang = freq_ref[...] * pos
    c = jnp.cos(ang)
    s = jnp.sin(ang)
    ts = pos.shape[-1]
    lo = jnp.zeros((ROPE_X1, ts), F32)
    mid = jnp.zeros((ROPE_X2 - ROPE_X1 - ROPE_HALF, ts), F32)
    hi = jnp.zeros((LANES - ROPE_X2 - ROPE_HALF, ts), F32)
    cos_ref[...] = jnp.concatenate([lo, c, mid, c, hi], axis=0).T
    sin_ref[...] = jnp.concatenate([lo, -s, mid, s, hi], axis=0).T


def _rope_tables(positions):
    b, s = positions.shape
    ts = _row_tile(s, ROPE_TILE)
    inv_freq = ROPE_THETA ** (-jnp.arange(0, MLA_ROPE, 2, dtype=F32) / MLA_ROPE)
    out = jax.ShapeDtypeStruct((b * s, LANES), F32)
    ns = s // ts
    return pl.pallas_call(
        _rope_kernel,
        grid=(b, ns),
        in_specs=[pl.BlockSpec((None, 1, ts), lambda i, j: (i, 0, j)), _full((ROPE_HALF, 1))],
        out_specs=[pl.BlockSpec((ts, LANES), lambda i, j: (i * ns + j, 0))] * 2,
        out_shape=[out, out],
        compiler_params=_params("parallel", "parallel"),
        name="rope_tables",
    )(positions.reshape(b, 1, s), inv_freq.reshape(ROPE_HALF, 1))


def _in_proj_kernel(x_ref, nw_ref, w_ref, cw_ref, cb_ref, z_ref, xs_ref, bc_ref, u_ref, cq_ref, ckv_ref,
                    small_ref, tail_scr, *, tiles_per_seq):
    tm = x_ref.shape[0]

    @pl.when(pl.program_id(0) % tiles_per_seq == 0)
    def _():
        tail_scr[...] = jnp.zeros_like(tail_scr)

    h = (_rms_scale(x_ref[...]) * nw_ref[...]).astype(BF16)
    proj = jnp.dot(h, w_ref[...], preferred_element_type=F32)
    z_ref[...] = proj[:, _C_Z:_C_XBC].astype(BF16)
    xbc = proj[:, _C_XBC:_C_U]
    xe = jnp.concatenate([tail_scr[...], xbc], axis=0)
    tail_scr[...] = xbc[tm - TAIL:, :]
    cw = cw_ref[...]
    conv = cb_ref[...] + cw[SSD_CONV - 1:SSD_CONV] * xbc
    for j in range(1, SSD_CONV):
        conv = conv + cw[SSD_CONV - 1 - j:SSD_CONV - j] * pltpu.roll(xe, j, axis=0)[TAIL:]
    act = _silu(conv).astype(BF16)
    xs_ref[...] = act[:, :SSD_WIDTH]
    bc_ref[...] = act[:, SSD_WIDTH:]
    u_ref[...] = proj[:, _C_U:_C_CQ].astype(BF16)
    cq_ref[...] = proj[:, _C_CQ:_C_CKV].astype(BF16)
    ckv_ref[...] = proj[:, _C_CKV:_C_SMALL].astype(BF16)
    small_ref[...] = proj[:, _C_SMALL:_C_END]


def _pack_w_in(w_in):
    d = w_in.shape[0]
    o = 0
    parts = {}
    for name, n in (("z", SSD_WIDTH), ("xbc", SSD_CONV_CH), ("dt", SSD_HEADS), ("u", POOL_WIDTH),
                    ("cq", MLA_Q_RANK), ("ckv", MLA_KV_RANK), ("kpe", MLA_ROPE)):
        parts[name] = w_in[:, o:o + n]
        o += n
    small = jnp.concatenate([
        parts["dt"], jnp.zeros((d, ROPE_X1 - SSD_HEADS), F32),
        parts["kpe"][:, :ROPE_HALF], jnp.zeros((d, ROPE_X2 - ROPE_X1 - ROPE_HALF), F32),
        parts["kpe"][:, ROPE_HALF:], jnp.zeros((d, LANES - ROPE_X2 - ROPE_HALF), F32)], axis=1)
    return jnp.concatenate([parts["z"], parts["xbc"], parts["u"], parts["cq"], parts["ckv"], small],
                           axis=1).astype(BF16)


def _in_proj(x, norm_w, w_packed, conv_w, conv_b, seq):
    t, d = x.shape
    tm = _row_tile(seq, PROJ_TILE)
    assert SSD_CONV - 1 < TAIL
    bc_w = SSD_CONV_CH - SSD_WIDTH
    widths = (SSD_WIDTH, SSD_WIDTH, bc_w, POOL_WIDTH, MLA_Q_RANK, MLA_KV_RANK)
    row = lambda n: pl.BlockSpec((tm, n), lambda i: (i, 0))
    return pl.pallas_call(
        functools.partial(_in_proj_kernel, tiles_per_seq=seq // tm),
        grid=(t // tm,),
        in_specs=[row(d), _full((1, d)), _resident((d, _C_END)),
                  _full((SSD_CONV, SSD_CONV_CH)), _full((1, SSD_CONV_CH))],
        out_specs=[row(n) for n in widths] + [row(LANES)],
        out_shape=[jax.ShapeDtypeStruct((t, n), BF16) for n in widths]
        + [jax.ShapeDtypeStruct((t, LANES), F32)],
        scratch_shapes=[pltpu.VMEM((TAIL, SSD_CONV_CH), F32)],
        compiler_params=_params("arbitrary"),
        name="in_proj",
    )(x, norm_w.reshape(1, d), w_packed, conv_w, conv_b.reshape(1, -1))


SSD_CHUNKS_PER_STEP = 8
SPLIT = 3


def _pack_split(v):
    out = None
    r = v
    for t in range(SPLIT):
        part = r.astype(BF16).astype(F32)
        r = r - part
        placed = part if t == 0 else pltpu.roll(part, t * SSD_HEADS, axis=1)
        out = placed if out is None else out + placed
    return out.astype(BF16)


def _unpack_split(r):
    out = r
    for t in range(1, SPLIT):
        out = out + pltpu.roll(r, LANES - t * SSD_HEADS, axis=1)
    return out


def _ssd_kernel(xs_ref, bc_ref, z_ref, small_ref, dtb_ref, alog_ref, dskip_ref, nw_ref, spread_ref, bcast_ref,
                y_ref, state_scr):
    L, N, P, E = SSD_CHUNK, SSD_STATE, SSD_HEAD_DIM, SSD_HEADS_PER_GROUP
    GW = E * P

    @pl.when(pl.program_id(1) == 0)
    def _():
        state_scr[...] = jnp.zeros_like(state_scr)

    lane = lax.broadcasted_iota(jnp.int32, (L, LANES), 1)
    row = lax.broadcasted_iota(jnp.int32, (L, LANES), 0)
    causal = lane <= row
    head_lane = lane < SSD_HEADS
    tri = jnp.where(causal, 1.0, 0.0).astype(BF16)
    a_neg = -jnp.exp(alog_ref[...])

    for sub in range(SSD_CHUNKS_PER_STEP):
        rows = slice(sub * L, (sub + 1) * L)
        xs_b = xs_ref[rows, :]
        xs = xs_b.astype(F32)
        pre = small_ref[rows, :] + dtb_ref[...]
        softplus = jnp.maximum(pre, 0.0) + jnp.log1p(jnp.exp(-jnp.abs(pre)))
        dt = jnp.where(head_lane, softplus, 0.0)
        da = dt * a_neg
        cum = jnp.where(head_lane, _unpack_split(
            jnp.dot(tri, _pack_split(da), preferred_element_type=F32)), 0.0)
        cum_t = cum.T
        cum_last = cum[L - 1:L, :]
        dte = jnp.where(head_lane, jnp.exp(cum_last - cum), 0.0)
        ecum = jnp.where(head_lane, jnp.exp(cum), 0.0)
        spread = jnp.dot(jnp.concatenate([_pack_split(dt), _pack_split(dte), _pack_split(ecum)], axis=0),
                         spread_ref[...], preferred_element_type=F32)
        dt_x, dte_x, ecum_x = spread[0:L], spread[L:2 * L], spread[2 * L:3 * L]
        cum_b = jnp.dot(_pack_split(cum), bcast_ref[...], preferred_element_type=F32)
        xdt = xs * dt_x
        xdt_b = xdt.astype(BF16)
        xdt_end = (xdt * dte_x).astype(BF16)

        y_off, y_diag = [], []
        for g in range(SSD_GROUPS):
            b_g = bc_ref[rows, g * N:(g + 1) * N]
            c_g = bc_ref[rows, SSD_GROUPS * N + g * N:SSD_GROUPS * N + (g + 1) * N]
            cb = lax.dot_general(c_g, b_g, (((1,), (1,)), ((), ())), preferred_element_type=F32)
            state = state_scr[g]
            y_off.append(jnp.dot(c_g, state.astype(BF16), preferred_element_type=F32))
            for e in range(0, E, 2):
                h = g * E + e
                ms = []
                for hh in (h, h + 1):
                    seg = cum_b[:, hh * L:(hh + 1) * L] - cum_t[hh:hh + 1, :]
                    ms.append((cb * jnp.exp(jnp.where(causal, seg, -jnp.inf))).astype(BF16))
                x2 = xdt_b[:, h * P:(h + 2) * P]
                zero = jnp.zeros_like(x2)
                rhs = jnp.concatenate([jnp.where(lane < P, x2, zero), jnp.where(lane < P, zero, x2)], axis=0)
                y_diag.append(jnp.dot(jnp.concatenate(ms, axis=1), rhs, preferred_element_type=F32))
            new = jnp.dot(b_g.astype(F32).T.astype(BF16), xdt_end[:, g * GW:(g + 1) * GW],
                          preferred_element_type=F32)
            state_scr[g] = state * ecum_x[L - 1:L, g * GW:(g + 1) * GW] + new

        y = jnp.concatenate(y_diag, axis=1) + jnp.concatenate(y_off, axis=1) * ecum_x + xs * dskip_ref[...]
        gated = y * _silu(z_ref[rows, :].astype(F32))
        y_ref[rows, :] = (_rms_scale(gated) * nw_ref[...]).astype(BF16)


def _ssd(xs, bc, z, small, dt_bias, a_log, d_skip, norm_w, batch, seq):
    L = SSD_CHUNK
    rows = L * math.gcd(SSD_CHUNKS_PER_STEP, seq // L)
    assert rows == L * SSD_CHUNKS_PER_STEP and SPLIT * SSD_HEADS <= LANES
    ns = seq // rows
    pad = lambda v: jnp.pad(v, (0, LANES - SSD_HEADS)).reshape(1, LANES)
    k = np.arange(LANES)
    valid = (k < SPLIT * SSD_HEADS)[:, None]
    spread = (valid & ((k % SSD_HEADS)[:, None] == (np.arange(SSD_WIDTH) // SSD_HEAD_DIM)[None, :]))
    bcast = (valid & ((k % SSD_HEADS)[:, None] == (np.arange(SSD_HEADS * L) // L)[None, :]))
    row = lambda n: pl.BlockSpec((rows, n), lambda b, c: (b * ns + c, 0))
    return pl.pallas_call(
        _ssd_kernel,
        grid=(batch, ns),
        in_specs=[row(SSD_WIDTH), row(SSD_CONV_CH - SSD_WIDTH), row(SSD_WIDTH), row(LANES),
                  _full((1, LANES)), _full((1, LANES)), _full((1, SSD_WIDTH)), _full((1, SSD_WIDTH)),
                  _full((LANES, SSD_WIDTH)), _full((LANES, SSD_HEADS * L))],
        out_specs=row(SSD_WIDTH),
        out_shape=jax.ShapeDtypeStruct((batch * seq, SSD_WIDTH), BF16),
        scratch_shapes=[pltpu.VMEM((SSD_GROUPS, SSD_STATE, SSD_HEADS_PER_GROUP * SSD_HEAD_DIM), F32)],
        compiler_params=_params("parallel", "arbitrary"),
        name="ssd",
    )(xs, bc, z, small, pad(dt_bias), pad(a_log), jnp.repeat(d_skip, SSD_HEAD_DIM).reshape(1, -1),
      norm_w.reshape(1, -1), jnp.asarray(spread, BF16), jnp.asarray(bcast, BF16))


def _pool_tile(u_ref, halo_ref, pw_ref, ps_ref, start):
    tm = u_ref.shape[0]
    x = u_ref[...].astype(F32)
    halo = jnp.where(start == 0, 0.0, halo_ref[...].astype(F32))
    xe = jnp.concatenate([halo, x], axis=0)
    pos = start + lax.broadcasted_iota(jnp.int32, (tm, 1), 0)
    out = []
    for gi, w in enumerate(POOL_WINDOWS):
        cols = slice(gi * POOL_GROUP_DIM, (gi + 1) * POOL_GROUP_DIM)
        acc = xe[:, cols]
        sh = 1
        while sh < w:
            acc = acc + pltpu.roll(acc, sh, axis=0)
            sh *= 2
        cnt = jnp.minimum(pos + 1, w).astype(F32)
        pooled = acc[HALO:] / cnt - x[:, cols]
        yg = jnp.dot(pooled.astype(BF16), pw_ref[gi], preferred_element_type=F32)
        out.append((yg * ps_ref[:, cols]).astype(BF16))
    return jnp.concatenate(out, axis=1)


def _rotate_half(x):
    return pltpu.roll(x, LANES // 2, axis=1)


def _mla_prep_kernel(cq_ref, ckv_ref, small_ref, cos_ref, sin_ref, qn_ref, kvn_ref,
                     wq_ref, wk_ref, wvt_ref, q_ref, k_ref, vt_ref, *, scale):
    tm = cq_ref.shape[0]
    lane = lax.broadcasted_iota(jnp.int32, (tm, LANES), 1)
    cos = cos_ref[...]
    sin = sin_ref[...]
    rope_lane = ((lane >= ROPE_X1) & (lane < ROPE_X1 + ROPE_HALF)) | ((lane >= ROPE_X2) & (lane < ROPE_X2 + ROPE_HALF))
    cos_q = jnp.where(rope_lane, cos, 1.0)
    qn = (_rms_scale(cq_ref[...].astype(F32)) * qn_ref[...]).astype(BF16)
    kvn = (_rms_scale(ckv_ref[...].astype(F32)) * kvn_ref[...]).astype(BF16)
    q = jnp.dot(qn, wq_ref[...], preferred_element_type=F32)
    k = jnp.dot(kvn, wk_ref[...], preferred_element_type=F32)
    vt_ref[...] = lax.dot_general(wvt_ref[...], kvn, (((1,), (1,)), ((), ())),
                                  preferred_element_type=F32).astype(BF16)
    kpe = small_ref[...]
    kpe = kpe * cos + _rotate_half(kpe) * sin
    for h in range(MLA_HEADS):
        cols = slice(h * MLA_HEAD_PAD, (h + 1) * MLA_HEAD_PAD)
        qh = q[:, cols]
        qh = qh * cos_q + _rotate_half(qh) * sin
        q_ref[h] = (qh * scale).astype(BF16)
        k_ref[h] = (k[:, cols] + kpe).astype(BF16)


def _head_layout(nope, x1, x2):
    r, h, _ = nope.shape
    gap = jnp.zeros((r, h, LANES // 2 - ROPE_X1 - ROPE_HALF), nope.dtype)
    if x1 is None:
        x1 = x2 = jnp.zeros((r, h, ROPE_HALF), nope.dtype)
    half = MLA_NOPE // 2
    return jnp.concatenate([nope[:, :, :half], x1, gap, nope[:, :, half:], x2, gap], axis=2).reshape(r, h * LANES)


def _pack_mla_weights(w_uq, w_ukv):
    assert ROPE_X1 == MLA_NOPE // 2
    rq = w_uq.shape[0]
    wq = w_uq.reshape(rq, MLA_HEADS, MLA_QK)
    wq = _head_layout(wq[:, :, :MLA_NOPE], wq[:, :, MLA_NOPE:MLA_NOPE + ROPE_HALF], wq[:, :, MLA_NOPE + ROPE_HALF:])
    rk = w_ukv.shape[0]
    wkv = w_ukv.reshape(rk, MLA_HEADS, MLA_NOPE + MLA_V)
    wk = _head_layout(wkv[:, :, :MLA_NOPE], None, None)
    wvt = wkv[:, :, MLA_NOPE:].reshape(rk, MLA_WIDTH).T
    return wq.astype(BF16), wk.astype(BF16), wvt.astype(BF16)


def _mla_prep(cq, ckv, small, cos_t, sin_t, q_norm, kv_norm, wq, wk, wvt, batch, seq):
    t = cq.shape[0]
    tm = _row_tile(seq, PROJ_TILE)
    ns = seq // tm
    qk_w = MLA_HEADS * MLA_HEAD_PAD
    row = lambda n: pl.BlockSpec((tm, n), lambda i: (i, 0))
    heads = pl.BlockSpec((MLA_HEADS, tm, MLA_HEAD_PAD), lambda i: (0, i, 0))
    return pl.pallas_call(
        functools.partial(_mla_prep_kernel, scale=math.log2(math.e) / math.sqrt(MLA_QK)),
        grid=(t // tm,),
        in_specs=[row(MLA_Q_RANK), row(MLA_KV_RANK), row(LANES), row(LANES), row(LANES),
                  _full((1, MLA_Q_RANK)), _full((1, MLA_KV_RANK)),
                  _full((MLA_Q_RANK, qk_w)), _full((MLA_KV_RANK, qk_w)), _full((MLA_WIDTH, MLA_KV_RANK))],
        out_specs=[heads, heads,
                   pl.BlockSpec((None, MLA_WIDTH, tm), lambda i: (i // ns, 0, i % ns))],
        out_shape=[jax.ShapeDtypeStruct((MLA_HEADS, t, MLA_HEAD_PAD), BF16),
                   jax.ShapeDtypeStruct((MLA_HEADS, t, MLA_HEAD_PAD), BF16),
                   jax.ShapeDtypeStruct((batch, MLA_WIDTH, seq), BF16)],
        compiler_params=_params("parallel"),
        name="mla_prep",
    )(cq, ckv, small, cos_t, sin_t, q_norm.reshape(1, -1), kv_norm.reshape(1, -1), wq, wk, wvt)


HEADS_PER_STEP = 8


LOOKAHEAD = 6
PV_TILE = 256
V_AUG = MLA_V + 16


def _flash_kernel(qi_ref, kj_ref, q_ref, k_ref, vt_ref, o_ref, m_scr, acc_scr, s_scr):
    p = pl.program_id(2)
    i = qi_ref[p]
    j = kj_ref[p]
    tq, tk = q_ref.shape[1], k_ref.shape[1]

    @pl.when(j == 0)
    def _():
        m_scr[...] = jnp.full_like(m_scr, -1e30)
        acc_scr[...] = jnp.zeros_like(acc_scr)

    def step(masked):
        ones = jnp.ones((V_AUG - MLA_V, tk), BF16)
        half = tq // 2
        m_half = {}

        def visible(c):
            return min(tk, (c + 1) * half) if masked else tk

        def scores(hh, c):
            n0 = c * half
            kr = visible(c)
            q = q_ref[hh, n0:n0 + half, :]
            k = k_ref[hh, 0:kr, :]
            s = lax.dot_general(k, q, (((1,), (1,)), ((), ())), preferred_element_type=F32)
            if masked:
                key = lax.broadcasted_iota(jnp.int32, (kr, half), 0)
                qry = lax.broadcasted_iota(jnp.int32, (kr, half), 1) + n0
                s = jnp.where(key <= qry, s, -1e30)
            s_scr[hh, c, 0:kr] = s
            m_half[(hh, c)] = jnp.max(s, axis=0, keepdims=True)

        def update(hh, c):
            cols = slice(c * half, (c + 1) * half)
            vt = jnp.concatenate([vt_ref[hh * MLA_V:(hh + 1) * MLA_V, :], ones], axis=0)
            m_prev = m_scr[hh, :, cols]
            m_new = jnp.maximum(m_prev, m_half[(hh, c)])
            alpha = jnp.exp2(m_prev - m_new)
            upd = None
            for k0 in range(0, visible(c), PV_TILE):
                pexp = jnp.exp2(s_scr[hh, c, k0:k0 + PV_TILE, :] - m_new)
                d = jnp.dot(vt[:, k0:k0 + PV_TILE], pexp.astype(BF16), preferred_element_type=F32)
                upd = d if upd is None else upd + d
            acc_scr[hh, :, cols] = alpha * acc_scr[hh, :, cols] + upd
            m_scr[hh, :, cols] = m_new

        units = [(hh, c) for hh in range(HEADS_PER_STEP) for c in range(2)]
        for u in units[:LOOKAHEAD]:
            scores(*u)
        for t, u in enumerate(units):
            if t + LOOKAHEAD < len(units):
                scores(*units[t + LOOKAHEAD])
            update(*u)

    @pl.when(j < i)
    def _():
        step(False)

    @pl.when(j == i)
    def _():
        step(True)
        o_t = jnp.concatenate([acc_scr[hh, :MLA_V] / acc_scr[hh, MLA_V:MLA_V + 1]
                               for hh in range(HEADS_PER_STEP)], axis=0)
        o_ref[...] = o_t.T.astype(BF16)


def _flash(q, k, vt, batch, seq):
    tq = _row_tile(seq, ATTN_TILE)
    nq = seq // tq
    pairs = [(i, j) for i in range(nq) for j in range(i + 1)]
    qi = jnp.asarray(np.array([p[0] for p in pairs], np.int32))
    kj = jnp.asarray(np.array([p[1] for p in pairs], np.int32))
    v_w = HEADS_PER_STEP * MLA_V
    grid_spec = pltpu.PrefetchScalarGridSpec(
        num_scalar_prefetch=2,
        grid=(batch, MLA_HEADS // HEADS_PER_STEP, len(pairs)),
        in_specs=[pl.BlockSpec((HEADS_PER_STEP, tq, MLA_HEAD_PAD), lambda b, h, p, qi, kj: (h, b * nq + qi[p], 0)),
                  pl.BlockSpec((HEADS_PER_STEP, tq, MLA_HEAD_PAD), lambda b, h, p, qi, kj: (h, b * nq + kj[p], 0)),
                  pl.BlockSpec((None, v_w, tq), lambda b, h, p, qi, kj: (b, h, kj[p]))],
        out_specs=pl.BlockSpec((tq, v_w), lambda b, h, p, qi, kj: (b * nq + qi[p], h)),
        scratch_shapes=[pltpu.VMEM((HEADS_PER_STEP, 1, tq), F32),
                        pltpu.VMEM((HEADS_PER_STEP, V_AUG, tq), F32),
                        pltpu.VMEM((HEADS_PER_STEP, 2, tq, tq // 2), F32)])
    return pl.pallas_call(
        _flash_kernel,
        grid_spec=grid_spec,
        out_shape=jax.ShapeDtypeStruct((batch * seq, MLA_WIDTH), BF16),
        compiler_params=_params("parallel", "parallel", "arbitrary"),
        name="flash_attn",
    )(qi, kj, q, k, vt)


def _mix_ffn_kernel(ys_ref, u_ref, halo_ref, ym_ref, x_ref, wo_ref, pw_ref, ps_ref, nw_ref,
                    wg_ref, wv_ref, cw_ref, cb_ref, wdn_ref, fn_ref, o_ref, tail_scr,
                    *, tiles_per_seq, seq, dff, final_norm):
    tm = x_ref.shape[0]
    a, b = SSD_WIDTH, SSD_WIDTH + POOL_WIDTH

    @pl.when(pl.program_id(0) % tiles_per_seq == 0)
    def _():
        tail_scr[...] = jnp.zeros_like(tail_scr)

    y_pool = _pool_tile(u_ref, halo_ref, pw_ref, ps_ref, (pl.program_id(0) * tm) % seq)
    xn = x_ref[...]
    xn = xn + jnp.dot(ys_ref[...], wo_ref[0:a, :], preferred_element_type=F32)
    xn = xn + jnp.dot(y_pool, wo_ref[a:b, :], preferred_element_type=F32)
    xn = xn + jnp.dot(ym_ref[...], wo_ref[b:, :], preferred_element_type=F32)
    h = (_rms_scale(xn) * nw_ref[...]).astype(BF16)

    def up_conv(w_ref, c0):
        cols = slice(c0, c0 + dff)
        u = jnp.dot(h, w_ref[...], preferred_element_type=F32)
        ue = jnp.concatenate([tail_scr[:, cols], u], axis=0)
        tail_scr[:, cols] = u[tm - TAIL:, :]
        w = cw_ref[:, cols]
        y = cb_ref[:, cols] + w[FFN_CONV - 1:FFN_CONV] * u
        for j in range(1, FFN_CONV):
            y = y + w[FFN_CONV - 1 - j:FFN_CONV - j] * pltpu.roll(ue, j, axis=0)[TAIL:]
        return y

    act = (_silu(up_conv(wg_ref, 0)) * up_conv(wv_ref, dff)).astype(BF16)
    out = xn + jnp.dot(act, wdn_ref[...], preferred_element_type=F32)
    if final_norm:
        out = _rms_scale(out) * fn_ref[...]
    o_ref[...] = out


def _mix_ffn(y_ssd, u, y_mla, x, w_out, pool_w, pool_scale, norm_w, w_up, conv_w, conv_b, w_down, final_w,
             seq, final_norm):
    t, d = x.shape
    dff = w_down.shape[0]
    tm = _row_tile(seq, ROW_TILE)
    assert FFN_CONV - 1 < TAIL and max(POOL_WINDOWS) - 1 < HALO and tm % HALO == 0
    hb = tm // HALO
    row = lambda n: pl.BlockSpec((tm, n), lambda i: (i, 0))
    return pl.pallas_call(
        functools.partial(_mix_ffn_kernel, tiles_per_seq=seq // tm, seq=seq, dff=dff, final_norm=final_norm),
        grid=(t // tm,),
        in_specs=[row(SSD_WIDTH), row(POOL_WIDTH),
                  pl.BlockSpec((HALO, POOL_WIDTH), lambda i: (jnp.maximum(i * hb - 1, 0), 0)),
                  row(MLA_WIDTH), row(d), _resident(w_out.shape),
                  _full((POOL_GROUPS, POOL_GROUP_DIM, POOL_GROUP_DIM)), _full((1, POOL_WIDTH)), _full((1, d)),
                  _resident((d, dff)), _resident((d, dff)), _full((FFN_CONV, 2 * dff)), _full((1, 2 * dff)),
                  _resident((dff, d)), _full((1, d))],
        out_specs=row(d),
        out_shape=jax.ShapeDtypeStruct((t, d), F32),
        scratch_shapes=[pltpu.VMEM((TAIL, 2 * dff), F32)],
        compiler_params=_params("arbitrary"),
        name="mix_ffn",
    )(y_ssd, u, u, y_mla, x, w_out.astype(BF16), pool_w.astype(BF16), pool_scale.reshape(1, -1),
      norm_w.reshape(1, d), w_up[:, :dff].astype(BF16), w_up[:, dff:].astype(BF16), conv_w,
      conv_b.reshape(1, -1), w_down.astype(BF16), final_w.reshape(1, d))


def kernel(x, positions, attn_norm, w_in, ssd_conv_w, ssd_conv_b, ssd_dt_bias, ssd_a_log, ssd_d, ssd_norm,
           pool_w, pool_scale, mla_q_norm, mla_w_uq, mla_kv_norm, mla_w_ukv, w_out, ffn_norm, ffn_w_up,
           ffn_conv_w, ffn_conv_b, ffn_w_down, final_norm):
    batch, seq, d = x.shape
    depth = w_in.shape[0]
    assert seq % SSD_CHUNK == 0
    cos_t, sin_t = _rope_tables(positions)
    xf = x.reshape(batch * seq, d)
    for l in range(depth):
        z, xs, bc, u, cq, ckv, small = _in_proj(xf, attn_norm[l], _pack_w_in(w_in[l]),
                                                ssd_conv_w[l], ssd_conv_b[l], seq)
        y_ssd = _ssd(xs, bc, z, small, ssd_dt_bias[l], ssd_a_log[l], ssd_d[l], ssd_norm[l], batch, seq)
        wq, wk, wvt = _pack_mla_weights(mla_w_uq[l], mla_w_ukv[l])
        q, k, vt = _mla_prep(cq, ckv, small, cos_t, sin_t, mla_q_norm[l], mla_kv_norm[l], wq, wk, wvt,
                             batch, seq)
        y_mla = _flash(q, k, vt, batch, seq)
        xf = _mix_ffn(y_ssd, u, y_mla, xf, w_out[l], pool_w[l], pool_scale[l], ffn_norm[l], ffn_w_up[l],
                      ffn_conv_w[l], ffn_conv_b[l], ffn_w_down[l], final_norm, seq, final_norm=(l == depth - 1))
    return xf.reshape(batch, seq, d)
```
